```python
import math
import jax, jax.numpy as jnp
from jax import lax
import numpy as np

D_MODEL = 2048
BATCH = 8
SEQ = 2048
DEPTH = 1

MEM_LEN = 256
HEAD_DIM = 128
NSA_HEADS = 8
NSA_KV_HEADS = 2
NSA_GROUP = NSA_HEADS // NSA_KV_HEADS
NSA_WIDTH = NSA_HEADS * HEAD_DIM
NSA_KV_WIDTH = NSA_KV_HEADS * HEAD_DIM
CMP_LEN = 32
CMP_STRIDE = 16
CMP_HIDDEN = 2 * HEAD_DIM
SLC_LEN = 64
SLC_TOPK = 16
WINDOW = 512
SLC_QBLK = 32
WIN_QBLK = 128
POOL_WINDOWS = (2, 4, 8, 16)
POOL_GROUPS = 4
POOL_CH = 128
POOL_WIDTH = POOL_GROUPS * POOL_CH
MEM_HEADS = 4
MEM_HEAD_DIM = 128
MEM_WIDTH = MEM_HEADS * MEM_HEAD_DIM
MIX_WIDTH = NSA_WIDTH + POOL_WIDTH + MEM_WIDTH
IN_SPLITS = (NSA_WIDTH,
             NSA_KV_WIDTH, NSA_KV_WIDTH,
             NSA_KV_WIDTH, NSA_KV_WIDTH,
             NSA_KV_WIDTH, NSA_KV_WIDTH,
             3 * NSA_HEADS,
             POOL_WIDTH,
             MEM_WIDTH)
IN_WIDTH = sum(IN_SPLITS)
N_EXPERTS = 32
TOP_K = 4
D_EXPERT = 2048
SWIGLU_ALPHA = 1.702
SWIGLU_LIMIT = 7.0
MOE_BLK = 128
LN_EPS = 1e-5
DN_ALPHA = (2 * DEPTH) ** 0.25
DN_BETA = (8 * DEPTH) ** -0.25
NEG_INF = -1e30
BIG = 1e9

kernel_name = "hybrid_nsa_pool_mem_moe_deepnorm"


def _layer_norm(x, g, b):
    xf = x.astype(jnp.float32)
    mu = jnp.mean(xf, -1, keepdims=True)
    var = jnp.mean(jnp.square(xf - mu), -1, keepdims=True)
    return ((xf - mu) * lax.rsqrt(var + LN_EPS) * g + b).astype(x.dtype)


def _rms_norm(x, g):
    xf = x.astype(jnp.float32)
    y = xf * lax.rsqrt(jnp.mean(jnp.square(xf), -1, keepdims=True) + LN_EPS)
    return (y * g).astype(x.dtype)


def _masked_softmax(s, mask):
    s = jnp.where(mask, s.astype(jnp.float32), NEG_INF)
    return jax.nn.softmax(s, axis=-1) * mask


def _compress(kv, pos, w1, w2):
    B, S = kv.shape[:2]
    n_c = (S - CMP_LEN) // CMP_STRIDE + 1
    idx = jnp.arange(n_c)[:, None] * CMP_STRIDE + jnp.arange(CMP_LEN)[None, :]
    blk = kv[:, idx] + pos[None, None, :, None, :]
    blk = jnp.moveaxis(blk, 3, 2).reshape(B, n_c, NSA_KV_HEADS, CMP_LEN * HEAD_DIM)
    return jax.nn.gelu(blk @ w1) @ w2


def _nsa(q, k_c, v_c, k_s, v_s, k_w, v_w, gates,
         cmp_pos_k, cmp_w1_k, cmp_w2_k, cmp_pos_v, cmp_w1_v, cmp_w2_v):
    B, S = q.shape[:2]
    KVH, G, dh = NSA_KV_HEADS, NSA_GROUP, HEAD_DIM
    scale = dh ** -0.5
    q = q.reshape(B, S, KVH, G, dh)
    k_c, v_c, k_s, v_s, k_w, v_w = [a.reshape(B, S, KVH, dh) for a in (k_c, v_c, k_s, v_s, k_w, v_w)]
    t = jnp.arange(S)

    n_c = (S - CMP_LEN) // CMP_STRIDE + 1
    kc = _compress(k_c, cmp_pos_k, cmp_w1_k, cmp_w2_k)
    vc = _compress(v_c, cmp_pos_v, cmp_w1_v, cmp_w2_v)
    c_start = jnp.arange(n_c) * CMP_STRIDE
    mask_c = ((c_start + CMP_LEN - 1)[None, :] <= t[:, None])[None, :, None, None, :]
    s_c = jnp.einsum('bshgd,bchd->bshgc', q, kc) * scale
    p_c = _masked_softmax(s_c, mask_c)
    o_cmp = jnp.einsum('bshgc,bchd->bshgd', p_c.astype(vc.dtype), vc)

    n_s = S // SLC_LEN
    s_start = jnp.arange(n_s) * SLC_LEN
    overlap = jnp.clip(jnp.minimum((c_start + CMP_LEN)[:, None], (s_start + SLC_LEN)[None, :])
                       - jnp.maximum(c_start[:, None], s_start[None, :]), 0)
    overlap = (overlap // CMP_STRIDE).astype(jnp.float32)
    imp = jnp.einsum('bshc,cj->bshj', jnp.sum(p_c, axis=3), overlap)
    cur = t // SLC_LEN
    j = jnp.arange(n_s)
    valid = j[None, :] <= cur[:, None]
    forced = (j[None, :] == 0) | (j[None, :] == cur[:, None]) | (j[None, :] == cur[:, None] - 1)
    imp = jnp.where(forced[None, :, None, :], BIG,
                    jnp.where(valid[None, :, None, :], imp, -BIG))
    k_sel = min(SLC_TOPK, n_s)
    sel = lax.top_k(imp, k_sel)[1]

    k_blocks = k_s.reshape(B, n_s, SLC_LEN, KVH, dh).transpose(0, 3, 1, 2, 4)
    v_blocks = v_s.reshape(B, n_s, SLC_LEN, KVH, dh).transpose(0, 3, 1, 2, 4)
    nq = S // SLC_QBLK
    bi = jnp.arange(B)[:, None, None, None]
    hi = jnp.arange(KVH)[None, None, :, None]

    def slc_block(args):
        qb, selb, tb = args
        kb = k_blocks[bi, hi, selb]
        vb = v_blocks[bi, hi, selb]
        kpos = selb[..., None] * SLC_LEN + jnp.arange(SLC_LEN)
        mask = (kpos <= tb[None, :, None, None, None]).reshape(B, SLC_QBLK, KVH, 1, k_sel * SLC_LEN)
        s = jnp.einsum('bqhgd,bqhkld->bqhgkl', qb, kb) * scale
        p = _masked_softmax(s.reshape(B, SLC_QBLK, KVH, G, k_sel * SLC_LEN), mask)
        return jnp.einsum('bqhgn,bqhnd->bqhgd', p.astype(vb.dtype),
                          vb.reshape(B, SLC_QBLK, KVH, k_sel * SLC_LEN, dh))

    q_blk = q.reshape(B, nq, SLC_QBLK, KVH, G, dh).swapaxes(0, 1)
    sel_blk = sel.reshape(B, nq, SLC_QBLK, KVH, k_sel).swapaxes(0, 1)
    o_slc = lax.map(slc_block, (q_blk, sel_blk, t.reshape(nq, SLC_QBLK)))
    o_slc = o_slc.swapaxes(0, 1).reshape(B, S, KVH, G, dh)

    nb = S // WIN_QBLK
    span = WIN_QBLK + WINDOW
    kw_pad = jnp.pad(k_w, ((0, 0), (WINDOW, 0), (0, 0), (0, 0)))
    vw_pad = jnp.pad(v_w, ((0, 0), (WINDOW, 0), (0, 0), (0, 0)))

    def win_block(args):
        i, qb = args
        start = i * WIN_QBLK
        kb = lax.dynamic_slice_in_dim(kw_pad, start, span, axis=1)
        vb = lax.dynamic_slice_in_dim(vw_pad, start, span, axis=1)
        kpos = start - WINDOW + jnp.arange(span)
        tq = start + jnp.arange(WIN_QBLK)
        mask = ((kpos[None, :] <= tq[:, None]) & (kpos[None, :] > tq[:, None] - WINDOW)
                & (kpos[None, :] >= 0))[None, :, None, None, :]
        s = jnp.einsum('bqhgd,bkhd->bqhgk', qb, kb) * scale
        p = _masked_softmax(s, mask)
        return jnp.einsum('bqhgk,bkhd->bqhgd', p.astype(vb.dtype), vb)

    qw = q.reshape(B, nb, WIN_QBLK, KVH, G, dh).swapaxes(0, 1)
    o_win = lax.map(win_block, (jnp.arange(nb), qw)).swapaxes(0, 1).reshape(B, S, KVH, G, dh)

    g = jax.nn.sigmoid(gates.astype(jnp.float32)).reshape(B, S, KVH, G, 3).astype(q.dtype)
    o = g[..., 0:1] * o_cmp + g[..., 1:2] * o_slc + g[..., 2:3] * o_win
    return o.reshape(B, S, NSA_WIDTH)


def _pool_mixer(u, pool_w, pool_scale):
    B, S = u.shape[:2]
    uf = u.astype(jnp.float32).reshape(B, S, POOL_GROUPS, POOL_CH)
    cs = jnp.pad(jnp.cumsum(uf, axis=1), ((0, 0), (1, 0), (0, 0), (0, 0)))
    win = jnp.array(POOL_WINDOWS, dtype=jnp.int32)
    t = jnp.arange(S)
    lo = jnp.maximum(t[:, None] + 1 - win[None, :], 0)
    cnt = (t[:, None] + 1 - lo).astype(jnp.float32)
    gidx = jnp.arange(POOL_GROUPS)[None, :]
    mean = (cs[:, 1:] - cs[:, lo, gidx]) / cnt[None, :, :, None]
    pooled = (mean - uf).astype(u.dtype)
    y = jnp.einsum('bsgc,gcd->bsgd', pooled, pool_w) * pool_scale.reshape(POOL_GROUPS, POOL_CH)
    return y.reshape(B, S, POOL_WIDTH)


def _mem_attn(qm, mem, w_mem_kv):
    B, S = qm.shape[:2]
    M = mem.shape[1]
    km, vm = jnp.split(mem @ w_mem_kv, 2, axis=-1)
    qm = qm.reshape(B, S, MEM_HEADS, MEM_HEAD_DIM)
    km = km.reshape(B, M, MEM_HEADS, MEM_HEAD_DIM)
    vm = vm.reshape(B, M, MEM_HEADS, MEM_HEAD_DIM)
    s = jnp.einsum('bshd,bmhd->bhsm', qm, km) * (MEM_HEAD_DIM ** -0.5)
    p = jax.nn.softmax(s.astype(jnp.float32), axis=-1).astype(vm.dtype)
    return jnp.einsum('bhsm,bmhd->bshd', p, vm).reshape(B, S, MEM_WIDTH)


def _moe(h, w_router, b_router, w_gu, b_gu, w_dn, b_dn):
    B, S, D = h.shape
    T = B * S
    hf = h.reshape(T, D)
    logits = (hf @ w_router + b_router).astype(jnp.float32)
    top_val, top_idx = lax.top_k(logits, TOP_K)
    gate = jax.nn.softmax(top_val, axis=-1)
    A = T * TOP_K
    flat_e = top_idx.reshape(A)
    flat_tok = jnp.repeat(jnp.arange(T, dtype=jnp.int32), TOP_K)
    flat_g = gate.reshape(A)
    order = jnp.argsort(flat_e)
    se = flat_e[order]
    counts = jnp.bincount(flat_e, length=N_EXPERTS)
    starts = jnp.cumsum(counts) - counts
    pcounts = (counts + MOE_BLK - 1) // MOE_BLK * MOE_BLK
    pends = jnp.cumsum(pcounts)
    pstarts = pends - pcounts
    dest = pstarts[se] + (jnp.arange(A) - starts[se])
    n_pad = A + N_EXPERTS * MOE_BLK
    nblk = n_pad // MOE_BLK
    tok_buf = jnp.zeros((n_pad,), jnp.int32).at[dest].set(flat_tok[order])
    g_buf = jnp.zeros((n_pad,), jnp.float32).at[dest].set(flat_g[order])
    blk_e = jnp.minimum(jnp.searchsorted(pends, jnp.arange(nblk) * MOE_BLK, side='right'),
                        N_EXPERTS - 1)

    def expert_block(args):
        tok, g, e = args
        xb = hf[tok]
        gu = xb @ w_gu[e] + b_gu[e]
        x_glu = jnp.minimum(gu[:, ::2], SWIGLU_LIMIT)
        x_lin = jnp.clip(gu[:, 1::2], -SWIGLU_LIMIT, SWIGLU_LIMIT)
        act = x_glu * jax.nn.sigmoid(SWIGLU_ALPHA * x_glu) * (x_lin + 1)
        y = act @ w_dn[e] + b_dn[e]
        return y * g[:, None].astype(y.dtype)

    ys = lax.map(expert_block, (tok_buf.reshape(nblk, MOE_BLK), g_buf.reshape(nblk, MOE_BLK), blk_e))
    out = jnp.zeros((T, D), h.dtype).at[tok_buf].add(ys.reshape(n_pad, D))
    return out.reshape(B, S, D)


def _hybrid_layer(x, mem, w_in, cmp_pos_k, cmp_w1_k, cmp_w2_k, cmp_pos_v, cmp_w1_v, cmp_w2_v,
                  pool_w, pool_scale, w_mem_kv, gn_nsa, gn_pool, gn_mem, w_out, ln1_g, ln1_b,
                  w_router, b_router, w_gu, b_gu, w_dn, b_dn, ln2_g, ln2_b):
    proj = x @ w_in
    offs = [int(o) for o in np.cumsum(IN_SPLITS)[:-1]]
    q, k_c, v_c, k_s, v_s, k_w, v_w, gates, u_pool, q_mem = jnp.split(proj, offs, axis=-1)
    o_nsa = _nsa(q, k_c, v_c, k_s, v_s, k_w, v_w, gates,
                 cmp_pos_k, cmp_w1_k, cmp_w2_k, cmp_pos_v, cmp_w1_v, cmp_w2_v)
    o_pool = _pool_mixer(u_pool, pool_w, pool_scale)
    o_mem = _mem_attn(q_mem, mem, w_mem_kv)
    mixed = jnp.concatenate([_rms_norm(o_nsa, gn_nsa), _rms_norm(o_pool, gn_pool),
                             _rms_norm(o_mem, gn_mem)], axis=-1) @ w_out
    h = _layer_norm(DN_ALPHA * x + mixed, ln1_g, ln1_b)
    ffn = _moe(h, w_router, b_router, w_gu, b_gu, w_dn, b_dn)
    return _layer_norm(DN_ALPHA * h + ffn, ln2_g, ln2_b)


def setup_inputs(seed: int = 0) -> dict:
    key = jax.random.key(seed)
    ks = iter(jax.random.split(key, 32))
    L = DEPTH

    def nrm(shape, scale):
        return jax.random.normal(next(ks), shape, jnp.float32) * scale

    v_cols = (2, 4, 6)
    col_scale = jnp.concatenate([jnp.full((n,), DN_BETA if i in v_cols else 1.0, jnp.float32)
                                 for i, n in enumerate(IN_SPLITS)])
    mem_scale = jnp.concatenate([jnp.ones((MEM_WIDTH,), jnp.float32),
                                 jnp.full((MEM_WIDTH,), DN_BETA, jnp.float32)])
    cin = CMP_LEN * HEAD_DIM
    return {
        "x": nrm((BATCH, SEQ, D_MODEL), 1.0),
        "mem": nrm((BATCH, MEM_LEN, D_MODEL), 1.0),
        "w_in": nrm((L, D_MODEL, IN_WIDTH), D_MODEL ** -0.5) * col_scale,
        "cmp_pos_k": nrm((L, CMP_LEN, HEAD_DIM), 0.1),
        "cmp_w1_k": nrm((L, cin, CMP_HIDDEN), cin ** -0.5),
        "cmp_w2_k": nrm((L, CMP_HIDDEN, HEAD_DIM), CMP_HIDDEN ** -0.5),
        "cmp_pos_v": nrm((L, CMP_LEN, HEAD_DIM), 0.1),
        "cmp_w1_v": nrm((L, cin, CMP_HIDDEN), cin ** -0.5),
        "cmp_w2_v": nrm((L, CMP_HIDDEN, HEAD_DIM), CMP_HIDDEN ** -0.5),
        "pool_w": nrm((L, POOL_GROUPS, POOL_CH, POOL_CH), POOL_CH ** -0.5),
        "pool_scale": 1.0 + nrm((L, POOL_WIDTH), 0.1),
        "w_mem_kv": nrm((L, D_MODEL, 2 * MEM_WIDTH), D_MODEL ** -0.5) * mem_scale,
        "gn_nsa": 1.0 + nrm((L, NSA_WIDTH), 0.05),
        "gn_pool": 1.0 + nrm((L, POOL_WIDTH), 0.05),
        "gn_mem": 1.0 + nrm((L, MEM_WIDTH), 0.05),
        "w_out": nrm((L, MIX_WIDTH, D_MODEL), MIX_WIDTH ** -0.5) * DN_BETA,
        "ln1_g": 1.0 + nrm((L, D_MODEL), 0.05),
        "ln1_b": nrm((L, D_MODEL), 0.02),
        "w_router": nrm((L, D_MODEL, N_EXPERTS), D_MODEL ** -0.5),
        "b_router": nrm((L, N_EXPERTS), 0.01),
        "w_gu": nrm((L, N_EXPERTS, D_MODEL, 2 * D_EXPERT), D_MODEL ** -0.5) * DN_BETA,
        "b_gu": nrm((L, N_EXPERTS, 2 * D_EXPERT), 0.01),
        "w_dn": nrm((L, N_EXPERTS, D_EXPERT, D_MODEL), D_EXPERT ** -0.5) * DN_BETA,
        "b_dn": nrm((L, N_EXPERTS, D_MODEL), 0.01),
        "ln2_g": 1.0 + nrm((L, D_MODEL), 0.05),
        "ln2_b": nrm((L, D_MODEL), 0.02),
    }


def reference(x, mem, w_in, cmp_pos_k, cmp_w1_k, cmp_w2_k, cmp_pos_v, cmp_w1_v, cmp_w2_v,
              pool_w, pool_scale, w_mem_kv, gn_nsa, gn_pool, gn_mem, w_out, ln1_g, ln1_b,
              w_router, b_router, w_gu, b_gu, w_dn, b_dn, ln2_g, ln2_b):
    for l in range(DEPTH):
        x = _hybrid_layer(x, mem, w_in[l], cmp_pos_k[l], cmp_w1_k[l], cmp_w2_k[l],
                          cmp_pos_v[l], cmp_w1_v[l], cmp_w2_v[l], pool_w[l], pool_scale[l],
                          w_mem_kv[l], gn_nsa[l], gn_pool[l], gn_mem[l], w_out[l],
                          ln1_g[l], ln1_b[l], w_router[l], b_router[l], w_gu[l], b_gu[l],
                          w_dn[l], b_dn[l], ln2_g[l], ln2_b[l])
    return x
```

```python
import functools

import jax
import jax.numpy as jnp
from jax import lax
from jax.experimental import pallas as pl
from jax.experimental.pallas import tpu as pltpu

D_MODEL = 2048
HEAD_DIM = 128
NSA_HEADS = 8
NSA_KV_HEADS = 2
NSA_GROUP = NSA_HEADS // NSA_KV_HEADS
NSA_WIDTH = NSA_HEADS * HEAD_DIM
NSA_KV_WIDTH = NSA_KV_HEADS * HEAD_DIM
CMP_LEN = 32
CMP_STRIDE = 16
CMP_HIDDEN = 2 * HEAD_DIM
SLC_LEN = 64
SLC_TOPK = 16
WINDOW = 512
POOL_WINDOWS = (2, 4, 8, 16)
POOL_GROUPS = 4
POOL_CH = 128
POOL_WIDTH = POOL_GROUPS * POOL_CH
MEM_HEADS = 4
MEM_HEAD_DIM = 128
MEM_WIDTH = MEM_HEADS * MEM_HEAD_DIM
N_EXPERTS = 32
TOP_K = 4
D_EXPERT = 2048
SWIGLU_ALPHA = 1.702
SWIGLU_LIMIT = 7.0
LN_EPS = 1e-5
NEG_INF = -1e30
BIG = 1e9

LANES = 128
SUBLANES = 8
VMEM_LIMIT = 56 * 1024 * 1024

COL_Q = 0
COL_KC = 1024
COL_KS = 1536
COL_KW = 2048
COL_POOL = 2560
COL_QMEM = 3072
COL_GATE = 3584
PROJ_WIDTH = 3840
GATE_SRC = 2560

NSA_TQ = 128
NSA_CK = 512
MIX_TT = 256
ROUTE_TT = 512
DISP_TT = 256
MOE_TM = 1024
MOE_SUB = 256
MOE_NSUB = MOE_TM // MOE_SUB
MOE_TF = 512
COMB_TT = 128

F32 = jnp.float32
BF16 = jnp.bfloat16
U32 = jnp.uint32
I32 = jnp.int32


def _dot(a, b):
    return jnp.dot(a, b, preferred_element_type=F32)


def _dot_nt(a, b):
    return lax.dot_general(a, b, (((1,), (1,)), ((), ())), preferred_element_type=F32)


def _params(*sem):
    return pltpu.CompilerParams(dimension_semantics=sem, vmem_limit_bytes=VMEM_LIMIT)


def _mm_kernel(x_ref, w_ref, o_ref, xb_ref):
    @pl.when(pl.program_id(1) == 0)
    def _():
        xb_ref[...] = x_ref[...].astype(BF16)

    o_ref[...] = _dot(xb_ref[...], w_ref[...].astype(BF16))


def _matmul(x, w, tm, tn):
    m, k = x.shape
    n = w.shape[1]
    return pl.pallas_call(
        _mm_kernel,
        out_shape=jax.ShapeDtypeStruct((m, n), F32),
        grid=(m // tm, n // tn),
        in_specs=[pl.BlockSpec((tm, k), lambda i, j: (i, 0)),
                  pl.BlockSpec((k, tn), lambda i, j: (0, j))],
        out_specs=pl.BlockSpec((tm, tn), lambda i, j: (i, j)),
        scratch_shapes=[pltpu.VMEM((tm, k), BF16)],
        compiler_params=_params("parallel", "arbitrary"),
    )(x, w)


def _gelu_tanh(x):
    return 0.5 * x * (1.0 + jnp.tanh(0.7978845608028654 * (x + 0.044715 * (x * x * x))))


def _cmp_kernel(kv_ref, pos_ref, w1_ref, w2_ref, o_ref):
    nchunk = kv_ref.shape[1] // CMP_STRIDE
    top = jnp.zeros((nchunk, CMP_HIDDEN), F32)
    bot = jnp.zeros((nchunk, CMP_HIDDEN), F32)
    for l in range(CMP_STRIDE):
        rows = kv_ref[0, pl.ds(l, nchunk, stride=CMP_STRIDE), :]
        a = (rows + pos_ref[0, l:l + 1, :]).astype(BF16)
        b = (rows + pos_ref[0, CMP_STRIDE + l:CMP_STRIDE + l + 1, :]).astype(BF16)
        top = top + _dot(a, w1_ref[0, l * HEAD_DIM:(l + 1) * HEAD_DIM, :].astype(BF16))
        bot = bot + _dot(b, w1_ref[0, (CMP_STRIDE + l) * HEAD_DIM:(CMP_STRIDE + l + 1) * HEAD_DIM, :].astype(BF16))
    hidden = top + pltpu.roll(bot, nchunk - 1, 0)
    act = _gelu_tanh(hidden)
    o_ref[0, 0] = _dot(act.astype(BF16), w2_ref[0].astype(BF16))


def _compress(proj, pos, w1, w2):
    b, s, _ = proj.shape
    nchunk = s // CMP_STRIDE
    ncol = 2 * NSA_KV_HEADS
    col0 = COL_KC // HEAD_DIM
    return pl.pallas_call(
        _cmp_kernel,
        out_shape=jax.ShapeDtypeStruct((b, ncol, nchunk, HEAD_DIM), F32),
        grid=(b, ncol),
        in_specs=[pl.BlockSpec((1, s, HEAD_DIM), lambda i, c: (i, 0, col0 + c)),
                  pl.BlockSpec((1, CMP_LEN, HEAD_DIM), lambda i, c: (c // NSA_KV_HEADS, 0, 0)),
                  pl.BlockSpec((1, CMP_LEN * HEAD_DIM, CMP_HIDDEN), lambda i, c: (c // NSA_KV_HEADS, 0, 0)),
                  pl.BlockSpec((1, CMP_HIDDEN, HEAD_DIM), lambda i, c: (c // NSA_KV_HEADS, 0, 0))],
        out_specs=pl.BlockSpec((1, 1, nchunk, HEAD_DIM), lambda i, c: (i, c, 0, 0)),
        compiler_params=_params("parallel", "arbitrary"),
    )(proj, pos, w1, w2)


def _split3(x):
    hi = x.astype(BF16)
    r = x - hi.astype(F32)
    mid = r.astype(BF16)
    lo = (r - mid.astype(F32)).astype(BF16)
    return hi, mid, lo


def _nsa_kernel(q_ref, g_ref, kc_ref, vc_ref, ks_ref, vs_ref, kw_ref, vw_ref, o_ref,
                ksb, vsb, kwb, vwb):
    qi = pl.program_id(2)
    tq = NSA_TQ
    grp = NSA_GROUP
    rows = grp * tq
    scale = HEAD_DIM ** -0.5
    seq = ks_ref.shape[1]
    n_c = kc_ref.shape[2]
    n_s = seq // SLC_LEN

    @pl.when(qi == 0)
    def _():
        ksb[...] = ks_ref[0].astype(BF16)
        vsb[...] = vs_ref[0].astype(BF16)
        kwb[...] = kw_ref[0].astype(BF16)
        vwb[...] = vw_ref[0].astype(BF16)

    t0 = qi * tq
    q = q_ref[0]
    qb = jnp.concatenate([q[:, g * HEAD_DIM:(g + 1) * HEAD_DIM] for g in range(grp)], axis=0).astype(BF16)
    tok = t0 + lax.broadcasted_iota(I32, (tq, 1), 0)

    kc = kc_ref[0, 0].astype(BF16)
    vc = vc_ref[0, 0].astype(BF16)
    cidx = lax.broadcasted_iota(I32, (1, n_c), 1)
    mask_c = (cidx * CMP_STRIDE + (CMP_LEN - 1) <= tok) & (cidx < n_c - 1)
    s = (_dot_nt(qb, kc) * scale).reshape(grp, tq, n_c)
    s = jnp.where(mask_c[None], s, NEG_INF)
    m = jnp.max(s, axis=-1, keepdims=True)
    e = jnp.where(mask_c[None], jnp.exp(s - m), 0.0)
    den = jnp.sum(e, axis=-1, keepdims=True)
    p_c = e / jnp.where(den > 0.0, den, 1.0)
    o_cmp = _dot(p_c.reshape(rows, n_c).astype(BF16), vc)

    psum = p_c[0]
    for g in range(1, grp):
        psum = psum + p_c[g]
    jcol = lax.broadcasted_iota(I32, (n_s, 1), 0)
    crow = lax.broadcasted_iota(I32, (1, n_c), 1)
    ov = jnp.minimum(crow * CMP_STRIDE + CMP_LEN, jcol * SLC_LEN + SLC_LEN) - jnp.maximum(crow * CMP_STRIDE, jcol * SLC_LEN)
    ov = jnp.where(crow < n_c - 1, jnp.maximum(ov, 0) // CMP_STRIDE, 0).astype(F32).astype(BF16)
    hi, mid, lo = _split3(psum)
    imp = _dot_nt(ov, hi) + _dot_nt(ov, mid) + _dot_nt(ov, lo)
    tok_l = t0 + lax.broadcasted_iota(I32, (1, tq), 1)
    cur = tok_l // SLC_LEN
    forced = (jcol == 0) | (jcol == cur) | (jcol == cur - 1)
    imp = jnp.where(forced, BIG, jnp.where(jcol <= cur, imp, -BIG))
    rank = jnp.zeros((n_s, tq), F32)
    for i in range(n_s):
        row = imp[i:i + 1, :]
        beats = (row > imp) | ((row == imp) & (jcol > i))
        rank = rank + jnp.where(beats, 1.0, 0.0)
    sel_t = jnp.where(rank < float(min(SLC_TOPK, n_s)), 1.0, 0.0)
    sel_t = jnp.concatenate([sel_t, jnp.zeros((LANES - n_s, tq), F32)], axis=0)
    sel = sel_t.T.astype(BF16)

    ck = NSA_CK
    jrow = lax.broadcasted_iota(I32, (LANES, 1), 0)

    def slc_body(c, carry):
        m_i, l_i, acc = carry
        start = pl.multiple_of(c * ck, ck)
        kb = ksb[pl.ds(start, ck), :]
        vb = vsb[pl.ds(start, ck), :]
        key = c * ck + lax.broadcasted_iota(I32, (1, ck), 1)
        expand = jnp.where((key // SLC_LEN) == jrow, 1.0, 0.0).astype(BF16)
        km = (_dot(sel, expand) > 0.5) & (key <= tok)
        sc = (_dot_nt(qb, kb) * scale).reshape(grp, tq, ck)
        sc = jnp.where(km[None], sc, NEG_INF)
        m_new = jnp.maximum(m_i, jnp.max(sc, axis=-1, keepdims=True))
        alpha = jnp.exp(m_i - m_new)
        p = jnp.where(km[None], jnp.exp(sc - m_new), 0.0)
        l_new = alpha * l_i + jnp.sum(p, axis=-1, keepdims=True)
        pv = _dot(p.reshape(rows, ck).astype(BF16), vb).reshape(grp, tq, HEAD_DIM)
        return m_new, l_new, alpha * acc + pv

    n_chunks = (t0 + tq + ck - 1) // ck
    m0 = jnp.full((grp, tq, 1), NEG_INF, F32)
    l0 = jnp.zeros((grp, tq, 1), F32)
    a0 = jnp.zeros((grp, tq, HEAD_DIM), F32)
    _, l_s, acc_s = lax.fori_loop(0, n_chunks, slc_body, (m0, l0, a0))
    o_slc = acc_s / l_s

    span = WINDOW + tq
    wstart = pl.multiple_of(jnp.maximum(t0 - WINDOW, 0), tq)
    kb = kwb[pl.ds(wstart, span), :]
    vb = vwb[pl.ds(wstart, span), :]
    key = wstart + lax.broadcasted_iota(I32, (1, span), 1)
    mask_w = (key <= tok) & (key > tok - WINDOW)
    sw = (_dot_nt(qb, kb) * scale).reshape(grp, tq, span)
    sw = jnp.where(mask_w[None], sw, NEG_INF)
    mw = jnp.max(sw, axis=-1, keepdims=True)
    pw = jnp.where(mask_w[None], jnp.exp(sw - mw), 0.0)
    pw = pw / jnp.sum(pw, axis=-1, keepdims=True)
    o_win = _dot(pw.reshape(rows, span).astype(BF16), vb).reshape(grp, tq, HEAD_DIM)

    gate = jax.nn.sigmoid(g_ref[0])
    o_cmp = o_cmp.reshape(grp, tq, HEAD_DIM)
    for g in range(grp):
        o = (gate[:, 3 * g:3 * g + 1] * o_cmp[g] + gate[:, 3 * g + 1:3 * g + 2] * o_slc[g]
             + gate[:, 3 * g + 2:3 * g + 3] * o_win[g])
        o_ref[0, :, g * HEAD_DIM:(g + 1) * HEAD_DIM] = o


def _nsa(proj, cmp_kv):
    b, s, _ = proj.shape
    tq = NSA_TQ
    gw = NSA_GROUP * HEAD_DIM
    n_c = cmp_kv.shape[2]

    def col(c0):
        return lambda i, h, j: (i, 0, c0 // HEAD_DIM + h)

    kv_spec = lambda c0: pl.BlockSpec((1, s, HEAD_DIM), col(c0))
    return pl.pallas_call(
        _nsa_kernel,
        out_shape=jax.ShapeDtypeStruct((b, s, NSA_WIDTH), F32),
        grid=(b, NSA_KV_HEADS, s // tq),
        in_specs=[pl.BlockSpec((1, tq, gw), lambda i, h, j: (i, j, h)),
                  pl.BlockSpec((1, tq, LANES), lambda i, h, j: (i, j, COL_GATE // LANES + h)),
                  pl.BlockSpec((1, 1, n_c, HEAD_DIM), lambda i, h, j: (i, h, 0, 0)),
                  pl.BlockSpec((1, 1, n_c, HEAD_DIM), lambda i, h, j: (i, NSA_KV_HEADS + h, 0, 0)),
                  kv_spec(COL_KS), kv_spec(COL_KS + NSA_KV_WIDTH),
                  kv_spec(COL_KW), kv_spec(COL_KW + NSA_KV_WIDTH)],
        out_specs=pl.BlockSpec((1, tq, gw), lambda i, h, j: (i, j, h)),
        scratch_shapes=[pltpu.VMEM((s, HEAD_DIM), BF16) for _ in range(4)],
        compiler_params=_params("parallel", "parallel", "arbitrary"),
    )(proj, proj, cmp_kv, cmp_kv, proj, proj, proj, proj)


def _rms(x, g):
    return x * lax.rsqrt(jnp.mean(x * x, axis=-1, keepdims=True) + LN_EPS) * g


def _layer_norm(x, g, b):
    mu = jnp.mean(x, axis=-1, keepdims=True)
    xc = x - mu
    var = jnp.mean(xc * xc, axis=-1, keepdims=True)
    return xc * lax.rsqrt(var + LN_EPS) * g + b


def _mix_kernel(alpha, onsa_ref, u_ref, halo_ref, qm_ref, mkv_ref, pw_ref, ps_ref, gnn_ref, gnp_ref, gnm_ref,
                wout_ref, x_ref, lg_ref, lb_ref, wr_ref, br_ref,
                h_ref, hpk_ref, idx_ref, gt_ref, grow_ref):
    i = pl.program_id(1)
    tt = MIX_TT
    halo_n = halo_ref.shape[1]

    u = u_ref[0]
    halo = jnp.where(i > 0, halo_ref[0], 0.0)
    ext = jnp.concatenate([halo, u], axis=0)
    tok = i * tt + lax.broadcasted_iota(I32, (tt, 1), 0)
    pooled = []
    for g, win in enumerate(POOL_WINDOWS):
        w = ext[:, g * POOL_CH:(g + 1) * POOL_CH]
        step = 1
        while step < win:
            w = w + pltpu.roll(w, step, 0)
            step *= 2
        cnt = jnp.minimum(tok + 1, win).astype(F32)
        ug = u[:, g * POOL_CH:(g + 1) * POOL_CH]
        pg = (w[halo_n:, :] / cnt - ug).astype(BF16)
        yg = _dot(pg, pw_ref[g].astype(BF16)) * ps_ref[:, g * POOL_CH:(g + 1) * POOL_CH]
        pooled.append(yg)
    o_pool = jnp.concatenate(pooled, axis=-1)

    qm = qm_ref[0]
    mkv = mkv_ref[0]
    heads = []
    for hh in range(MEM_HEADS):
        sl = slice(hh * MEM_HEAD_DIM, (hh + 1) * MEM_HEAD_DIM)
        kh = mkv[:, sl].astype(BF16)
        vh = mkv[:, MEM_WIDTH + hh * MEM_HEAD_DIM:MEM_WIDTH + (hh + 1) * MEM_HEAD_DIM].astype(BF16)
        sc = _dot_nt(qm[:, sl].astype(BF16), kh) * (MEM_HEAD_DIM ** -0.5)
        sc = sc - jnp.max(sc, axis=-1, keepdims=True)
        p = jnp.exp(sc)
        p = p / jnp.sum(p, axis=-1, keepdims=True)
        heads.append(_dot(p.astype(BF16), vh))
    o_mem = jnp.concatenate(heads, axis=-1)

    cat = jnp.concatenate([_rms(onsa_ref[0], gnn_ref[...]), _rms(o_pool, gnp_ref[...]),
                           _rms(o_mem, gnm_ref[...])], axis=-1).astype(BF16)
    mixed = _dot(cat, wout_ref[...])
    h = _layer_norm(alpha * x_ref[0] + mixed, lg_ref[...], lb_ref[...])
    h_ref[...] = h

    hb = h.astype(BF16).astype(F32)
    half = D_MODEL // 2
    lo = pltpu.bitcast(hb[:, :half], U32) >> 16
    hi = pltpu.bitcast(hb[:, half:], U32) & jnp.uint32(0xFFFF0000)
    hpk_ref[...] = lo | hi

    h_hi = h.astype(BF16)
    h_lo = (h - h_hi.astype(F32)).astype(BF16)
    wr = wr_ref[...]
    w_hi = wr.astype(BF16)
    w_lo = (wr - w_hi.astype(F32)).astype(BF16)
    logits = _dot_nt(w_hi, h_hi) + _dot_nt(w_hi, h_lo) + _dot_nt(w_lo, h_hi) + br_ref[...]
    ecol = lax.broadcasted_iota(I32, (N_EXPERTS, 1), 0)
    vals, idxs = [], []
    work = logits
    for _ in range(TOP_K):
        mx = jnp.max(work, axis=0, keepdims=True)
        ix = jnp.min(jnp.where(work == mx, ecol, N_EXPERTS), axis=0, keepdims=True)
        vals.append(mx)
        idxs.append(ix)
        work = jnp.where(ecol == ix, -jnp.inf, work)
    ex = [jnp.exp(v - vals[0]) for v in vals]
    den = ex[0] + ex[1] + ex[2] + ex[3]
    gates = jnp.concatenate([e_ / den for e_ in ex], axis=0)
    idx_ref[...] = jnp.concatenate(idxs, axis=0)
    gt_ref[...] = gates
    gpad = jnp.concatenate([gates, jnp.zeros((LANES - TOP_K, tt), F32)], axis=0)
    grow_ref[...] = gpad.T


def _mix(alpha, onsa, proj, memkv, pool_w, pool_scale, gn_nsa, gn_pool, gn_mem, wout_b, x, ln_g, ln_b, wr_t, br):
    b, s, d = x.shape
    t = b * s
    tt = MIX_TT
    nt = s // tt
    halo = 16
    vec = lambda n: pl.BlockSpec((1, n), lambda i, j: (0, 0))
    return pl.pallas_call(
        functools.partial(_mix_kernel, alpha),
        out_shape=(jax.ShapeDtypeStruct((t, d), F32),
                   jax.ShapeDtypeStruct((t, d // 2), U32),
                   jax.ShapeDtypeStruct((TOP_K, t), I32),
                   jax.ShapeDtypeStruct((TOP_K, t), F32),
                   jax.ShapeDtypeStruct((t, LANES), F32)),
        grid=(b, nt),
        in_specs=[pl.BlockSpec((1, tt, NSA_WIDTH), lambda i, j: (i, j, 0)),
                  pl.BlockSpec((1, tt, POOL_WIDTH), lambda i, j: (i, j, COL_POOL // POOL_WIDTH)),
                  pl.BlockSpec((1, halo, POOL_WIDTH),
                               lambda i, j: (i, jnp.maximum(j * (tt // halo) - 1, 0), COL_POOL // POOL_WIDTH)),
                  pl.BlockSpec((1, tt, MEM_WIDTH), lambda i, j: (i, j, COL_QMEM // MEM_WIDTH)),
                  pl.BlockSpec((1, memkv.shape[1], 2 * MEM_WIDTH), lambda i, j: (i, 0, 0)),
                  pl.BlockSpec((POOL_GROUPS, POOL_CH, POOL_CH), lambda i, j: (0, 0, 0)),
                  vec(POOL_WIDTH), vec(NSA_WIDTH), vec(POOL_WIDTH), vec(MEM_WIDTH),
                  pl.BlockSpec((d, d), lambda i, j: (0, 0)),
                  pl.BlockSpec((1, tt, d), lambda i, j: (i, j, 0)),
                  vec(d), vec(d),
                  pl.BlockSpec((N_EXPERTS, d), lambda i, j: (0, 0)),
                  pl.BlockSpec((N_EXPERTS, 1), lambda i, j: (0, 0))],
        out_specs=(pl.BlockSpec((tt, d), lambda i, j: (i * nt + j, 0)),
                   pl.BlockSpec((tt, d // 2), lambda i, j: (i * nt + j, 0)),
                   pl.BlockSpec((TOP_K, tt), lambda i, j: (0, i * nt + j)),
                   pl.BlockSpec((TOP_K, tt), lambda i, j: (0, i * nt + j)),
                   pl.BlockSpec((tt, LANES), lambda i, j: (i * nt + j, 0))),
        compiler_params=_params("parallel", "arbitrary"),
    )(onsa, proj, proj, proj, memkv, pool_w, pool_scale, gn_nsa, gn_pool, gn_mem, wout_b, x, ln_g, ln_b, wr_t, br)


def _route_kernel(idx_ref, dest_ref, cnt_ref, run_ref, base_ref):
    ph = pl.program_id(0)
    i = pl.program_id(1)
    n = pl.num_programs(1)
    tt = ROUTE_TT
    ecol = lax.broadcasted_iota(I32, (N_EXPERTS, 1), 0)
    idx = idx_ref[...]
    onehot = jnp.zeros((N_EXPERTS, tt), F32)
    for k in range(TOP_K):
        onehot = onehot + jnp.where(idx[k:k + 1, :] == ecol, 1.0, 0.0)

    @pl.when(i == 0)
    def _():
        run_ref[...] = jnp.zeros_like(run_ref)

    @pl.when(ph == 0)
    def _():
        run_ref[...] = run_ref[...] + jnp.sum(onehot, axis=1, keepdims=True)

        @pl.when(i == n - 1)
        def _():
            cnt = run_ref[...]
            cnt_ref[...] = cnt
            pc = jnp.floor((cnt + (MOE_SUB - 1)) * (1.0 / MOE_SUB)) * MOE_SUB
            inc = pc
            d = 1
            while d < N_EXPERTS:
                inc = inc + jnp.where(ecol >= d, pltpu.roll(inc, d, 0), 0.0)
                d *= 2
            base_ref[...] = inc - pc

    @pl.when(ph == 1)
    def _():
        r = lax.broadcasted_iota(I32, (tt, tt), 0)
        c = lax.broadcasted_iota(I32, (tt, tt), 1)
        before = jnp.where(r < c, 1.0, 0.0).astype(BF16)
        pref = _dot(onehot.astype(BF16), before)
        pos = pref + run_ref[:, 0:1] + base_ref[:, 0:1]
        rows_ = [jnp.sum(jnp.where(idx[k:k + 1, :] == ecol, pos, 0.0), axis=0, keepdims=True) for k in range(TOP_K)]
        dest_ref[...] = jnp.concatenate(rows_, axis=0).astype(I32)
        run_ref[...] = run_ref[...] + jnp.sum(onehot, axis=1, keepdims=True)


def _route(idx_t):
    t = idx_t.shape[1]
    tt = ROUTE_TT
    return pl.pallas_call(
        _route_kernel,
        out_shape=(jax.ShapeDtypeStruct((TOP_K, t), I32),
                   jax.ShapeDtypeStruct((N_EXPERTS, LANES), F32)),
        grid=(2, t // tt),
        in_specs=[pl.BlockSpec((TOP_K, tt), lambda p, i: (0, i))],
        out_specs=(pl.BlockSpec((TOP_K, tt), lambda p, i: (0, i * p)),
                   pl.BlockSpec((N_EXPERTS, LANES), lambda p, i: (0, 0))),
        scratch_shapes=[pltpu.VMEM((N_EXPERTS, LANES), F32), pltpu.VMEM((N_EXPERTS, LANES), F32)],
        compiler_params=_params("arbitrary", "arbitrary"),
    )(idx_t)


def _dispatch_kernel(dest_ref, hpk_ref, init_ref, xs_ref, sem):
    del init_ref
    tt = DISP_TT

    def row_copy(t, k):
        return pltpu.make_async_copy(hpk_ref.at[pl.ds(t, 1)], xs_ref.at[pl.ds(dest_ref[k, t], 1)], sem)

    def issue(t, carry):
        for k in range(TOP_K):
            row_copy(t, k).start()
        return carry

    lax.fori_loop(0, tt, issue, 0)
    for _ in range(TOP_K):
        pltpu.make_async_copy(hpk_ref, xs_ref.at[pl.ds(0, tt)], sem).wait()


def _dispatch(dest, hpk, n_pad):
    t, w = hpk.shape
    tt = DISP_TT
    init = jnp.zeros((n_pad, w), U32)
    return pl.pallas_call(
        _dispatch_kernel,
        out_shape=jax.ShapeDtypeStruct((n_pad, w), U32),
        grid=(t // tt,),
        in_specs=[pl.BlockSpec((TOP_K, tt), lambda i: (0, i), memory_space=pltpu.SMEM),
                  pl.BlockSpec((tt, w), lambda i: (i, 0)),
                  pl.BlockSpec(memory_space=pl.ANY)],
        out_specs=pl.BlockSpec(memory_space=pl.ANY),
        scratch_shapes=[pltpu.SemaphoreType.DMA(())],
        input_output_aliases={2: 0},
        compiler_params=_params("arbitrary"),
    )(dest, hpk, init)


def _experts_kernel(tile_ref, exp_ref, lo_ref, hi_ref, first_ref,
                    xs_ref, wg_ref, wl_ref, bg_ref, bl_ref, wd_ref, bd_ref, y_ref, xb_ref):
    del tile_ref, exp_ref
    w = pl.program_id(0)
    j = pl.program_id(1)
    nj = pl.num_programs(1)
    lo = lo_ref[w]
    hi = hi_ref[w]
    half = D_MODEL // 2

    @pl.when((j == 0) & (first_ref[w] == 1))
    def _():
        y_ref[...] = jnp.zeros_like(y_ref)

    @pl.when((j == 0) & (first_ref[w] == 1) & (lo < hi))
    def _():
        word = xs_ref[...]
        xb_ref[:, :half] = pltpu.bitcast(word << 16, F32).astype(BF16)
        xb_ref[:, half:] = pltpu.bitcast(word & jnp.uint32(0xFFFF0000), F32).astype(BF16)

    for sblk in range(MOE_NSUB):
        @pl.when((lo <= sblk) & (sblk < hi))
        def _():
            r0 = sblk * MOE_SUB
            xb = xb_ref[r0:r0 + MOE_SUB, :]
            glu = jnp.minimum(_dot(xb, wg_ref[0]) + bg_ref[0], SWIGLU_LIMIT)
            lin = jnp.clip(_dot(xb, wl_ref[0]) + bl_ref[0], -SWIGLU_LIMIT, SWIGLU_LIMIT)
            act = glu * jax.nn.sigmoid(SWIGLU_ALPHA * glu) * (lin + 1.0)
            y = _dot(act.astype(BF16), wd_ref[0])
            y = y + jnp.where(j == nj - 1, 1.0, 0.0) * bd_ref[0]
            y_ref[r0:r0 + MOE_SUB, :] += y


def _experts(items, xs, wg, wl, bg, bl, wd, bd):
    n_pad = xs.shape[0]
    n_items = items[0].shape[0]
    nj = D_EXPERT // MOE_TF
    grid_spec = pltpu.PrefetchScalarGridSpec(
        num_scalar_prefetch=5,
        grid=(n_items, nj),
        in_specs=[pl.BlockSpec((MOE_TM, D_MODEL // 2), lambda w, j, tl, ex, lo, hi, fi: (tl[w], 0)),
                  pl.BlockSpec((1, D_MODEL, MOE_TF), lambda w, j, tl, ex, lo, hi, fi: (ex[w], 0, j)),
                  pl.BlockSpec((1, D_MODEL, MOE_TF), lambda w, j, tl, ex, lo, hi, fi: (ex[w], 0, j)),
                  pl.BlockSpec((1, 1, MOE_TF), lambda w, j, tl, ex, lo, hi, fi: (ex[w], 0, j)),
                  pl.BlockSpec((1, 1, MOE_TF), lambda w, j, tl, ex, lo, hi, fi: (ex[w], 0, j)),
                  pl.BlockSpec((1, MOE_TF, D_MODEL), lambda w, j, tl, ex, lo, hi, fi: (ex[w], j, 0)),
                  pl.BlockSpec((1, 1, D_MODEL), lambda w, j, tl, ex, lo, hi, fi: (ex[w], 0, 0))],
        out_specs=pl.BlockSpec((MOE_TM, D_MODEL), lambda w, j, tl, ex, lo, hi, fi: (tl[w], 0)),
        scratch_shapes=[pltpu.VMEM((MOE_TM, D_MODEL), BF16)],
    )
    return pl.pallas_call(
        _experts_kernel,
        out_shape=jax.ShapeDtypeStruct((n_pad, D_MODEL), F32),
        grid_spec=grid_spec,
        compiler_params=_params("arbitrary", "arbitrary"),
    )(*items, xs, wg, wl, bg, bl, wd, bd)


def _expert_items(counts, n_tiles):
    pc = (counts + MOE_SUB - 1) // MOE_SUB
    ends = jnp.cumsum(pc)
    starts = ends - pc
    t_lo = starts // MOE_NSUB
    t_hi = (ends + MOE_NSUB - 1) // MOE_NSUB
    n_it = jnp.where(pc > 0, t_hi - t_lo, 0)
    it_end = jnp.cumsum(n_it)
    it_start = it_end - n_it
    total = it_end[-1]
    n_items = n_tiles + N_EXPERTS - 1
    w = jnp.arange(n_items, dtype=I32)
    wc = jnp.minimum(w, total - 1)
    ex = jnp.minimum(jnp.searchsorted(it_end, wc, side='right'), N_EXPERTS - 1).astype(I32)
    tile = (t_lo[ex] + wc - it_start[ex]).astype(I32)
    live = w < total
    lo = jnp.where(live, jnp.maximum(starts[ex] - tile * MOE_NSUB, 0), 0).astype(I32)
    hi = jnp.where(live, jnp.minimum(ends[ex] - tile * MOE_NSUB, MOE_NSUB), 0).astype(I32)
    used = (ends[-1] + MOE_NSUB - 1) // MOE_NSUB
    spare = used + (w - total)
    tile = jnp.where(live, tile, jnp.minimum(spare, n_tiles - 1)).astype(I32)
    prev_tile = jnp.concatenate([jnp.full((1,), -1, I32), tile[:-1]])
    first = jnp.where(live, tile != prev_tile, spare < n_tiles).astype(I32)
    return tile, ex, lo, hi, first


def _combine_kernel(alpha, dcur_ref, dnxt_ref, ys_ref, h_ref, grow_ref, lg_ref, lb_ref, o_ref, buf, sem):
    i = pl.program_id(0)
    n = pl.num_programs(0)
    tt = COMB_TT
    slot = i % 2

    def issue(d_ref, s_):
        def body(t, carry):
            for k in range(TOP_K):
                pltpu.make_async_copy(ys_ref.at[pl.ds(d_ref[k, t], 1)], buf.at[s_, k, pl.ds(t, 1)], sem.at[s_]).start()
            return carry
        lax.fori_loop(0, tt, body, 0)

    @pl.when(i == 0)
    def _():
        issue(dcur_ref, 0)

    @pl.when(i + 1 < n)
    def _():
        issue(dnxt_ref, 1 - slot)

    for k in range(TOP_K):
        pltpu.make_async_copy(ys_ref.at[pl.ds(0, tt)], buf.at[slot, k], sem.at[slot]).wait()

    g = grow_ref[...]
    acc = g[:, 0:1] * buf[slot, 0]
    for k in range(1, TOP_K):
        acc = acc + g[:, k:k + 1] * buf[slot, k]
    o_ref[...] = _layer_norm(alpha * h_ref[...] + acc, lg_ref[...], lb_ref[...])


def _combine(alpha, dest, ys, h, grow, ln_g, ln_b):
    t, d = h.shape
    tt = COMB_TT
    n = t // tt
    return pl.pallas_call(
        functools.partial(_combine_kernel, alpha),
        out_shape=jax.ShapeDtypeStruct((t, d), F32),
        grid=(n,),
        in_specs=[pl.BlockSpec((TOP_K, tt), lambda i: (0, i), memory_space=pltpu.SMEM),
                  pl.BlockSpec((TOP_K, tt), lambda i: (0, jnp.minimum(i + 1, n - 1)), memory_space=pltpu.SMEM),
                  pl.BlockSpec(memory_space=pl.ANY),
                  pl.BlockSpec((tt, d), lambda i: (i, 0)),
                  pl.BlockSpec((tt, LANES), lambda i: (i, 0)),
                  pl.BlockSpec((1, d), lambda i: (0, 0)),
                  pl.BlockSpec((1, d), lambda i: (0, 0))],
        out_specs=pl.BlockSpec((tt, d), lambda i: (i, 0)),
        scratch_shapes=[pltpu.VMEM((2, TOP_K, tt, d), F32), pltpu.SemaphoreType.DMA((2,))],
        compiler_params=_params("arbitrary"),
    )(dest, dest, ys, h, grow, ln_g, ln_b)


def _regroup_w_in(w_in):
    gates = w_in[:, GATE_SRC:GATE_SRC + 3 * NSA_HEADS]
    per_head = 3 * NSA_GROUP
    gate_blocks = [jnp.pad(gates[:, h * per_head:(h + 1) * per_head], ((0, 0), (0, LANES - per_head)))
                   for h in range(NSA_KV_HEADS)]
    return jnp.concatenate([w_in[:, :GATE_SRC], w_in[:, GATE_SRC + 3 * NSA_HEADS:]] + gate_blocks, axis=1)


def _layer(depth, x, mem, w_in, cmp_pos_k, cmp_w1_k, cmp_w2_k, cmp_pos_v, cmp_w1_v, cmp_w2_v,
           pool_w, pool_scale, w_mem_kv, gn_nsa, gn_pool, gn_mem, w_out, ln1_g, ln1_b,
           w_router, b_router, w_gu, b_gu, w_dn, b_dn, ln2_g, ln2_b):
    b, s, d = x.shape
    t = b * s
    alpha = (2 * depth) ** 0.25
    row = lambda v: v.reshape(1, -1)

    proj = _matmul(x.reshape(t, d), _regroup_w_in(w_in), 1024, 768).reshape(b, s, PROJ_WIDTH)
    memkv = _matmul(mem.reshape(-1, d), w_mem_kv, 1024, 512).reshape(b, mem.shape[1], 2 * MEM_WIDTH)
    cmp_kv = _compress(proj, jnp.stack([cmp_pos_k, cmp_pos_v]), jnp.stack([cmp_w1_k, cmp_w1_v]),
                       jnp.stack([cmp_w2_k, cmp_w2_v]))
    onsa = _nsa(proj, cmp_kv)
    h, hpk, idx_t, _, grow = _mix(alpha, onsa, proj, memkv, pool_w, row(pool_scale), row(gn_nsa), row(gn_pool),
                                  row(gn_mem), w_out.astype(BF16), x, row(ln1_g), row(ln1_b),
                                  w_router.T, b_router.reshape(-1, 1))

    dest, cnt = _route(idx_t)
    n_pad = t * TOP_K + N_EXPERTS * MOE_SUB
    n_pad = (n_pad + MOE_TM - 1) // MOE_TM * MOE_TM
    xs = _dispatch(dest, hpk, n_pad)
    items = _expert_items(cnt[:, 0].astype(I32), n_pad // MOE_TM)
    ys = _experts(items, xs,
                  w_gu[:, :, 0::2].astype(BF16), w_gu[:, :, 1::2].astype(BF16),
                  b_gu[:, None, 0::2], b_gu[:, None, 1::2],
                  w_dn.astype(BF16), b_dn[:, None, :])
    out = _combine(alpha, dest, ys, h, grow, row(ln2_g), row(ln2_b))
    return out.reshape(b, s, d)


def kernel(x, mem, w_in, cmp_pos_k, cmp_w1_k, cmp_w2_k, cmp_pos_v, cmp_w1_v, cmp_w2_v, pool_w, pool_scale,
           w_mem_kv, gn_nsa, gn_pool, gn_mem, w_out, ln1_g, ln1_b, w_router, b_router, w_gu, b_gu, w_dn, b_dn,
           ln2_g, ln2_b):
    depth = w_in.shape[0]
    for l in range(depth):
        x = _layer(depth, x, mem, w_in[l], cmp_pos_k[l], cmp_w1_k[l], cmp_w2_k[l], cmp_pos_v[l], cmp_w1_v[l],
                   cmp_w2_v[l], pool_w[l], pool_scale[l], w_mem_kv[l], gn_nsa[l], gn_pool[l], gn_mem[l],
                   w_out[l], ln1_g[l], ln1_b[l], w_router[l], b_router[l], w_gu[l], b_gu[l], w_dn[l], b_dn[l],
                   ln2_g[l], ln2_b[l])
    return x
```

```python
import functools

import jax
import jax.numpy as jnp
from jax import lax
from jax.experimental import pallas as pl
from jax.experimental.pallas import tpu as pltpu

D_MODEL = 2048
HEAD_DIM = 128
NSA_HEADS = 8
NSA_KV_HEADS = 2
NSA_GROUP = NSA_HEADS // NSA_KV_HEADS
NSA_WIDTH = NSA_HEADS * HEAD_DIM
NSA_KV_WIDTH = NSA_KV_HEADS * HEAD_DIM
CMP_LEN = 32
CMP_STRIDE = 16
CMP_HIDDEN = 2 * HEAD_DIM
SLC_LEN = 64
SLC_TOPK = 16
WINDOW = 512
POOL_WINDOWS = (2, 4, 8, 16)
POOL_GROUPS = 4
POOL_CH = 128
POOL_WIDTH = POOL_GROUPS * POOL_CH
MEM_HEADS = 4
MEM_HEAD_DIM = 128
MEM_WIDTH = MEM_HEADS * MEM_HEAD_DIM
N_EXPERTS = 32
TOP_K = 4
D_EXPERT = 2048
SWIGLU_ALPHA = 1.702
SWIGLU_LIMIT = 7.0
LN_EPS = 1e-5
NEG_INF = -1e30
LOG2E = 1.4426950408889634
BIG = 1e9

LANES = 128
SUBLANES = 8
VMEM_LIMIT = 56 * 1024 * 1024

COL_Q = 0
COL_KC = 1024
COL_KS = 1536
COL_KW = 2048
COL_POOL = 2560
COL_QMEM = 3072
COL_GATE = 3584
PROJ_WIDTH = 3840
GATE_SRC = 2560

NSA_TQ = 256
NSA_CK = 512
MIX_TT = 512
ROUTE_TT = 512
DISP_TT = 256
MOE_TM = 1024
MOE_SUB = 256
MOE_NSUB = MOE_TM // MOE_SUB
MOE_TF = 512
DEGATE_BLK = 512
assert MOE_SUB % DISP_TT == 0 and MOE_TM % MOE_SUB == 0
COMB_TT = 128

F32 = jnp.float32
BF16 = jnp.bfloat16
U32 = jnp.uint32
I32 = jnp.int32


def _dot(a, b):
    return jnp.dot(a, b, preferred_element_type=F32)


def _dot_nt(a, b):
    return lax.dot_general(a, b, (((1,), (1,)), ((), ())), preferred_element_type=F32)


def _params(*sem):
    return pltpu.CompilerParams(dimension_semantics=sem, vmem_limit_bytes=VMEM_LIMIT)


def _mm_kernel(x_ref, w_ref, o_ref, xb_ref):
    @pl.when(pl.program_id(1) == 0)
    def _():
        xb_ref[...] = x_ref[...].astype(BF16)

    o_ref[...] = _dot(xb_ref[...], w_ref[...].astype(BF16))


def _matmul(x, w, tm, tn):
    m, k = x.shape
    n = w.shape[1]
    return pl.pallas_call(
        _mm_kernel,
        out_shape=jax.ShapeDtypeStruct((m, n), F32),
        grid=(m // tm, n // tn),
        in_specs=[pl.BlockSpec((tm, k), lambda i, j: (i, 0)),
                  pl.BlockSpec((k, tn), lambda i, j: (0, j))],
        out_specs=pl.BlockSpec((tm, tn), lambda i, j: (i, j)),
        scratch_shapes=[pltpu.VMEM((tm, k), BF16)],
        compiler_params=_params("parallel", "arbitrary"),
    )(x, w)


def _gelu_tanh(x):
    return 0.5 * x * (1.0 + jnp.tanh(0.7978845608028654 * (x + 0.044715 * (x * x * x))))


def _cmp_kernel(kv_ref, pos_ref, w1_ref, w2_ref, o_ref):
    nchunk = kv_ref.shape[1] // CMP_STRIDE
    top = jnp.zeros((nchunk, CMP_HIDDEN), F32)
    bot = jnp.zeros((nchunk, CMP_HIDDEN), F32)
    for l in range(CMP_STRIDE):
        rows = kv_ref[0, pl.ds(l, nchunk, stride=CMP_STRIDE), :]
        a = (rows + pos_ref[0, l:l + 1, :]).astype(BF16)
        b = (rows + pos_ref[0, CMP_STRIDE + l:CMP_STRIDE + l + 1, :]).astype(BF16)
        top = top + _dot(a, w1_ref[0, l * HEAD_DIM:(l + 1) * HEAD_DIM, :].astype(BF16))
        bot = bot + _dot(b, w1_ref[0, (CMP_STRIDE + l) * HEAD_DIM:(CMP_STRIDE + l + 1) * HEAD_DIM, :].astype(BF16))
    hidden = top + pltpu.roll(bot, nchunk - 1, 0)
    act = _gelu_tanh(hidden)
    o_ref[0, 0] = _dot(act.astype(BF16), w2_ref[0].astype(BF16))


def _compress(proj, pos, w1, w2):
    b, s, _ = proj.shape
    nchunk = s // CMP_STRIDE
    ncol = 2 * NSA_KV_HEADS
    col0 = COL_KC // HEAD_DIM
    return pl.pallas_call(
        _cmp_kernel,
        out_shape=jax.ShapeDtypeStruct((b, ncol, nchunk, HEAD_DIM), F32),
        grid=(b, ncol),
        in_specs=[pl.BlockSpec((1, s, HEAD_DIM), lambda i, c: (i, 0, col0 + c)),
                  pl.BlockSpec((1, CMP_LEN, HEAD_DIM), lambda i, c: (c // NSA_KV_HEADS, 0, 0)),
                  pl.BlockSpec((1, CMP_LEN * HEAD_DIM, CMP_HIDDEN), lambda i, c: (c // NSA_KV_HEADS, 0, 0)),
                  pl.BlockSpec((1, CMP_HIDDEN, HEAD_DIM), lambda i, c: (c // NSA_KV_HEADS, 0, 0))],
        out_specs=pl.BlockSpec((1, 1, nchunk, HEAD_DIM), lambda i, c: (i, c, 0, 0)),
        compiler_params=_params("parallel", "arbitrary"),
    )(proj, pos, w1, w2)


def _split3(x):
    hi = x.astype(BF16)
    r = x - hi.astype(F32)
    mid = r.astype(BF16)
    lo = (r - mid.astype(F32)).astype(BF16)
    return hi, mid, lo


def _nsa_kernel(q_ref, g_ref, kc_ref, vc_ref, ks_ref, vs_ref, kw_ref, vw_ref, o_ref,
                ksb, vsb, kwb, vwb, exp_ref):
    qi = pl.program_id(2)
    tq = NSA_TQ
    grp = NSA_GROUP
    rows = grp * tq
    seq = ks_ref.shape[1]
    n_c = kc_ref.shape[2]
    n_s = seq // SLC_LEN

    @pl.when(qi == 0)
    def _():
        ksb[...] = ks_ref[0].astype(BF16)
        vsb[...] = vs_ref[0].astype(BF16)
        kwb[...] = kw_ref[0].astype(BF16)
        vwb[...] = vw_ref[0].astype(BF16)
        blk = lax.broadcasted_iota(I32, (LANES, seq), 0)
        key_blk = lax.broadcasted_iota(I32, (LANES, seq), 1) // SLC_LEN
        exp_ref[...] = jnp.where(blk == key_blk, 1.0, 0.0).astype(BF16)

    t0 = qi * tq
    q = q_ref[0] * (HEAD_DIM ** -0.5 * LOG2E)
    qb = jnp.concatenate([q[:, g * HEAD_DIM:(g + 1) * HEAD_DIM] for g in range(grp)], axis=0).astype(BF16)
    tok = t0 + lax.broadcasted_iota(I32, (tq, 1), 0)

    span = WINDOW + tq
    wstart = pl.multiple_of(jnp.maximum(t0 - WINDOW, 0), tq)
    kb = kwb[pl.ds(wstart, span), :]
    vb = vwb[pl.ds(wstart, span), :]
    key = wstart + lax.broadcasted_iota(I32, (1, span), 1)
    bias_w = jnp.where((key <= tok) & (key > tok - WINDOW), 0.0, NEG_INF)
    sw = _dot_nt(qb, kb).reshape(grp, tq, span) + bias_w[None]
    pw = jnp.exp2(sw - jnp.max(sw, axis=-1, keepdims=True))
    l_w = jnp.sum(pw, axis=-1, keepdims=True)
    o_win = _dot(pw.reshape(rows, span).astype(BF16), vb).reshape(grp, tq, HEAD_DIM) / l_w

    kc = kc_ref[0, 0].astype(BF16)
    vc = vc_ref[0, 0].astype(BF16)
    cidx = lax.broadcasted_iota(I32, (1, n_c), 1)
    mask_c = (cidx * CMP_STRIDE + (CMP_LEN - 1) <= tok) & (cidx < n_c - 1)
    s = _dot_nt(qb, kc).reshape(grp, tq, n_c) + jnp.where(mask_c, 0.0, NEG_INF)[None]
    m = jnp.max(s, axis=-1, keepdims=True)
    any_c = jnp.where(tok >= CMP_LEN - 1, 1.0, 0.0)
    e = jnp.exp2(s - m) * any_c[None]
    den = jnp.sum(e, axis=-1, keepdims=True)
    p_c = e / jnp.where(den > 0.0, den, 1.0)
    o_cmp = _dot(p_c.reshape(rows, n_c).astype(BF16), vc)

    psum = p_c[0]
    for g in range(1, grp):
        psum = psum + p_c[g]
    jcol = lax.broadcasted_iota(I32, (n_s, 1), 0)
    crow = lax.broadcasted_iota(I32, (1, n_c), 1)
    ov = jnp.minimum(crow * CMP_STRIDE + CMP_LEN, jcol * SLC_LEN + SLC_LEN) - jnp.maximum(crow * CMP_STRIDE, jcol * SLC_LEN)
    ov = jnp.where(crow < n_c - 1, jnp.maximum(ov, 0) // CMP_STRIDE, 0).astype(F32).astype(BF16)
    hi, mid, lo = _split3(psum)
    imp = _dot_nt(ov, hi) + _dot_nt(ov, mid) + _dot_nt(ov, lo)
    tok_l = t0 + lax.broadcasted_iota(I32, (1, tq), 1)
    cur = tok_l // SLC_LEN
    forced = (jcol == 0) | (jcol == cur) | (jcol == cur - 1)
    imp = jnp.where(forced, BIG, jnp.where(jcol <= cur, imp, -BIG))
    rank = jnp.zeros((n_s, tq), F32)
    for i in range(n_s):
        row = imp[i:i + 1, :]
        beats = (row > imp) | ((row == imp) & (jcol > i))
        rank = rank + jnp.where(beats, 1.0, 0.0)
    sel_t = jnp.where(rank < float(min(SLC_TOPK, n_s)), 1.0, 0.0)
    sel_t = jnp.concatenate([sel_t, jnp.zeros((LANES - n_s, tq), F32)], axis=0)
    sel = sel_t.T.astype(BF16)

    ck = NSA_CK

    def slc_body(c, carry):
        m_i, l_i, acc = carry
        start = pl.multiple_of(c * ck, ck)
        kb = ksb[pl.ds(start, ck), :]
        vb = vsb[pl.ds(start, ck), :]
        key = c * ck + lax.broadcasted_iota(I32, (1, ck), 1)
        picked = _dot(sel, exp_ref[:, pl.ds(start, ck)])
        bias = jnp.where((picked > 0.5) & (key <= tok), 0.0, NEG_INF)
        sc = _dot_nt(qb, kb).reshape(grp, tq, ck) + bias[None]
        m_new = jnp.maximum(m_i, jnp.max(sc, axis=-1, keepdims=True))
        alpha = jnp.exp2(m_i - m_new)
        p = jnp.exp2(sc - m_new)
        l_new = alpha * l_i + jnp.sum(p, axis=-1, keepdims=True)
        pv = _dot(p.reshape(rows, ck).astype(BF16), vb).reshape(grp, tq, HEAD_DIM)
        return m_new, l_new, alpha * acc + pv

    n_chunks = (t0 + tq + ck - 1) // ck
    m0 = jnp.full((grp, tq, 1), NEG_INF, F32)
    l0 = jnp.zeros((grp, tq, 1), F32)
    a0 = jnp.zeros((grp, tq, HEAD_DIM), F32)
    _, l_s, acc_s = lax.fori_loop(0, n_chunks, slc_body, (m0, l0, a0))
    o_slc = acc_s / l_s

    gate = jax.nn.sigmoid(g_ref[0])
    o_cmp = o_cmp.reshape(grp, tq, HEAD_DIM)
    for g in range(grp):
        o = (gate[:, 3 * g:3 * g + 1] * o_cmp[g] + gate[:, 3 * g + 1:3 * g + 2] * o_slc[g]
             + gate[:, 3 * g + 2:3 * g + 3] * o_win[g])
        o_ref[0, :, g * HEAD_DIM:(g + 1) * HEAD_DIM] = o


def _nsa(proj, cmp_kv):
    b, s, _ = proj.shape
    tq = NSA_TQ
    gw = NSA_GROUP * HEAD_DIM
    n_c = cmp_kv.shape[2]

    def col(c0):
        return lambda i, h, j: (i, 0, c0 // HEAD_DIM + h)

    kv_spec = lambda c0: pl.BlockSpec((1, s, HEAD_DIM), col(c0))
    return pl.pallas_call(
        _nsa_kernel,
        out_shape=jax.ShapeDtypeStruct((b, s, NSA_WIDTH), F32),
        grid=(b, NSA_KV_HEADS, s // tq),
        in_specs=[pl.BlockSpec((1, tq, gw), lambda i, h, j: (i, j, h)),
                  pl.BlockSpec((1, tq, LANES), lambda i, h, j: (i, j, COL_GATE // LANES + h)),
                  pl.BlockSpec((1, 1, n_c, HEAD_DIM), lambda i, h, j: (i, h, 0, 0)),
                  pl.BlockSpec((1, 1, n_c, HEAD_DIM), lambda i, h, j: (i, NSA_KV_HEADS + h, 0, 0)),
                  kv_spec(COL_KS), kv_spec(COL_KS + NSA_KV_WIDTH),
                  kv_spec(COL_KW), kv_spec(COL_KW + NSA_KV_WIDTH)],
        out_specs=pl.BlockSpec((1, tq, gw), lambda i, h, j: (i, j, h)),
        scratch_shapes=[pltpu.VMEM((s, HEAD_DIM), BF16) for _ in range(4)] + [pltpu.VMEM((LANES, s), BF16)],
        compiler_params=_params("parallel", "parallel", "arbitrary"),
    )(proj, proj, cmp_kv, cmp_kv, proj, proj, proj, proj)


def _rms(x, g):
    return x * lax.rsqrt(jnp.mean(x * x, axis=-1, keepdims=True) + LN_EPS) * g


def _layer_norm(x, g, b):
    mu = jnp.mean(x, axis=-1, keepdims=True)
    xc = x - mu
    var = jnp.mean(xc * xc, axis=-1, keepdims=True)
    return xc * lax.rsqrt(var + LN_EPS) * g + b


def _mix_kernel(alpha, onsa_ref, u_ref, halo_ref, qm_ref, mkv_ref, pw_ref, ps_ref, gnn_ref, gnp_ref, gnm_ref,
                wout_ref, x_ref, lg_ref, lb_ref, wr_ref, br_ref,
                h_ref, idx_ref, gt_ref, grow_ref):
    i = pl.program_id(1)
    tt = MIX_TT
    halo_n = halo_ref.shape[1]

    u = u_ref[0]
    halo = jnp.where(i > 0, halo_ref[0], 0.0)
    ext = jnp.concatenate([halo, u], axis=0)
    tok = i * tt + lax.broadcasted_iota(I32, (tt, 1), 0)
    pooled = []
    for g, win in enumerate(POOL_WINDOWS):
        w = ext[:, g * POOL_CH:(g + 1) * POOL_CH]
        step = 1
        while step < win:
            w = w + pltpu.roll(w, step, 0)
            step *= 2
        cnt = jnp.minimum(tok + 1, win).astype(F32)
        ug = u[:, g * POOL_CH:(g + 1) * POOL_CH]
        pg = (w[halo_n:, :] / cnt - ug).astype(BF16)
        yg = _dot(pg, pw_ref[g].astype(BF16)) * ps_ref[:, g * POOL_CH:(g + 1) * POOL_CH]
        pooled.append(yg)
    o_pool = jnp.concatenate(pooled, axis=-1)

    qm = qm_ref[0]
    mkv = mkv_ref[0]
    heads = []
    for hh in range(MEM_HEADS):
        sl = slice(hh * MEM_HEAD_DIM, (hh + 1) * MEM_HEAD_DIM)
        kh = mkv[:, sl].astype(BF16)
        vh = mkv[:, MEM_WIDTH + hh * MEM_HEAD_DIM:MEM_WIDTH + (hh + 1) * MEM_HEAD_DIM].astype(BF16)
        sc = _dot_nt(qm[:, sl].astype(BF16), kh) * (MEM_HEAD_DIM ** -0.5)
        sc = sc - jnp.max(sc, axis=-1, keepdims=True)
        p = jnp.exp(sc)
        p = p / jnp.sum(p, axis=-1, keepdims=True)
        heads.append(_dot(p.astype(BF16), vh))
    o_mem = jnp.concatenate(heads, axis=-1)

    cat = jnp.concatenate([_rms(onsa_ref[0], gnn_ref[...]), _rms(o_pool, gnp_ref[...]),
                           _rms(o_mem, gnm_ref[...])], axis=-1).astype(BF16)
    mixed = _dot(cat, wout_ref[...])
    h = _layer_norm(alpha * x_ref[0] + mixed, lg_ref[...], lb_ref[...])
    h_ref[...] = h

    h_hi = h.astype(BF16)
    h_lo = (h - h_hi.astype(F32)).astype(BF16)
    wr = wr_ref[...]
    w_hi = wr.astype(BF16)
    w_lo = (wr - w_hi.astype(F32)).astype(BF16)
    logits = _dot_nt(w_hi, h_hi) + _dot_nt(w_hi, h_lo) + _dot_nt(w_lo, h_hi) + br_ref[...]
    ecol = lax.broadcasted_iota(I32, (N_EXPERTS, 1), 0)
    vals, idxs = [], []
    work = logits
    for _ in range(TOP_K):
        mx = jnp.max(work, axis=0, keepdims=True)
        ix = jnp.min(jnp.where(work == mx, ecol, N_EXPERTS), axis=0, keepdims=True)
        vals.append(mx)
        idxs.append(ix)
        work = jnp.where(ecol == ix, -jnp.inf, work)
    ex = [jnp.exp(v - vals[0]) for v in vals]
    den = ex[0] + ex[1] + ex[2] + ex[3]
    gates = jnp.concatenate([e_ / den for e_ in ex], axis=0)
    idx_ref[...] = jnp.concatenate(idxs, axis=0)
    gt_ref[...] = gates
    gpad = jnp.concatenate([gates, jnp.zeros((LANES - TOP_K, tt), F32)], axis=0)
    grow_ref[...] = gpad.T


def _mix(alpha, onsa, proj, memkv, pool_w, pool_scale, gn_nsa, gn_pool, gn_mem, wout_b, x, ln_g, ln_b, wr_t, br):
    b, s, d = x.shape
    t = b * s
    tt = MIX_TT
    nt = s // tt
    halo = 16
    vec = lambda n: pl.BlockSpec((1, n), lambda i, j: (0, 0))
    return pl.pallas_call(
        functools.partial(_mix_kernel, alpha),
        out_shape=(jax.ShapeDtypeStruct((t, d), F32),
                   jax.ShapeDtypeStruct((TOP_K, t), I32),
                   jax.ShapeDtypeStruct((TOP_K, t), F32),
                   jax.ShapeDtypeStruct((t, LANES), F32)),
        grid=(b, nt),
        in_specs=[pl.BlockSpec((1, tt, NSA_WIDTH), lambda i, j: (i, j, 0)),
                  pl.BlockSpec((1, tt, POOL_WIDTH), lambda i, j: (i, j, COL_POOL // POOL_WIDTH)),
                  pl.BlockSpec((1, halo, POOL_WIDTH),
                               lambda i, j: (i, jnp.maximum(j * (tt // halo) - 1, 0), COL_POOL // POOL_WIDTH)),
                  pl.BlockSpec((1, tt, MEM_WIDTH), lambda i, j: (i, j, COL_QMEM // MEM_WIDTH)),
                  pl.BlockSpec((1, memkv.shape[1], 2 * MEM_WIDTH), lambda i, j: (i, 0, 0)),
                  pl.BlockSpec((POOL_GROUPS, POOL_CH, POOL_CH), lambda i, j: (0, 0, 0)),
                  vec(POOL_WIDTH), vec(NSA_WIDTH), vec(POOL_WIDTH), vec(MEM_WIDTH),
                  pl.BlockSpec((d, d), lambda i, j: (0, 0)),
                  pl.BlockSpec((1, tt, d), lambda i, j: (i, j, 0)),
                  vec(d), vec(d),
                  pl.BlockSpec((N_EXPERTS, d), lambda i, j: (0, 0)),
                  pl.BlockSpec((N_EXPERTS, 1), lambda i, j: (0, 0))],
        out_specs=(pl.BlockSpec((tt, d), lambda i, j: (i * nt + j, 0)),
                   pl.BlockSpec((TOP_K, tt), lambda i, j: (0, i * nt + j)),
                   pl.BlockSpec((TOP_K, tt), lambda i, j: (0, i * nt + j)),
                   pl.BlockSpec((tt, LANES), lambda i, j: (i * nt + j, 0))),
        compiler_params=_params("parallel", "arbitrary"),
    )(onsa, proj, proj, proj, memkv, pool_w, pool_scale, gn_nsa, gn_pool, gn_mem, wout_b, x, ln_g, ln_b, wr_t, br)


def _route_kernel(idx_ref, dest_ref, cnt_ref, run_ref, base_ref):
    ph = pl.program_id(0)
    i = pl.program_id(1)
    n = pl.num_programs(1)
    tt = ROUTE_TT
    ecol = lax.broadcasted_iota(I32, (N_EXPERTS, 1), 0)
    idx = idx_ref[...]
    onehot = jnp.zeros((N_EXPERTS, tt), F32)
    for k in range(TOP_K):
        onehot = onehot + jnp.where(idx[k:k + 1, :] == ecol, 1.0, 0.0)

    @pl.when(i == 0)
    def _():
        run_ref[...] = jnp.zeros_like(run_ref)

    @pl.when(ph == 0)
    def _():
        run_ref[...] = run_ref[...] + jnp.sum(onehot, axis=1, keepdims=True)

        @pl.when(i == n - 1)
        def _():
            cnt = run_ref[...]
            cnt_ref[...] = cnt
            pc = jnp.floor((cnt + (MOE_SUB - 1)) * (1.0 / MOE_SUB)) * MOE_SUB
            inc = pc
            d = 1
            while d < N_EXPERTS:
                inc = inc + jnp.where(ecol >= d, pltpu.roll(inc, d, 0), 0.0)
                d *= 2
            base_ref[...] = inc - pc

    @pl.when(ph == 1)
    def _():
        r = lax.broadcasted_iota(I32, (tt, tt), 0)
        c = lax.broadcasted_iota(I32, (tt, tt), 1)
        before = jnp.where(r < c, 1.0, 0.0).astype(BF16)
        pref = _dot(onehot.astype(BF16), before)
        pos = pref + run_ref[:, 0:1] + base_ref[:, 0:1]
        rows_ = [jnp.sum(jnp.where(idx[k:k + 1, :] == ecol, pos, 0.0), axis=0, keepdims=True) for k in range(TOP_K)]
        dest_ref[...] = jnp.concatenate(rows_, axis=0).astype(I32)
        run_ref[...] = run_ref[...] + jnp.sum(onehot, axis=1, keepdims=True)


def _route(idx_t):
    t = idx_t.shape[1]
    tt = ROUTE_TT
    return pl.pallas_call(
        _route_kernel,
        out_shape=(jax.ShapeDtypeStruct((TOP_K, t), I32),
                   jax.ShapeDtypeStruct((N_EXPERTS, LANES), F32)),
        grid=(2, t // tt),
        in_specs=[pl.BlockSpec((TOP_K, tt), lambda p, i: (0, i))],
        out_specs=(pl.BlockSpec((TOP_K, tt), lambda p, i: (0, i * p)),
                   pl.BlockSpec((N_EXPERTS, LANES), lambda p, i: (0, 0))),
        scratch_shapes=[pltpu.VMEM((N_EXPERTS, LANES), F32), pltpu.VMEM((N_EXPERTS, LANES), F32)],
        compiler_params=_params("arbitrary", "arbitrary"),
    )(idx_t)


def _dispatch_kernel(pad_start_ref, pad_len_ref, tail_ref, dest_ref, h_ref, xs_ref, sem):
    tt = DISP_TT

    def row_copy(src_row, dst_row):
        return pltpu.make_async_copy(h_ref.at[pl.ds(src_row, 1)], xs_ref.at[pl.ds(dst_row, 1)], sem)

    def block_copy(dst_row):
        return pltpu.make_async_copy(h_ref, xs_ref.at[pl.ds(dst_row, tt)], sem)

    def issue(t, carry):
        for k in range(TOP_K):
            row_copy(t, dest_ref[k, t]).start()
        return carry

    lax.fori_loop(0, tt, issue, 0)

    @pl.when(pl.program_id(0) == 0)
    def _():
        def fill_expert(e, carry):
            def fill(r, c):
                row_copy(0, pad_start_ref[e] + r).start()
                return c
            return lax.fori_loop(0, pad_len_ref[e], fill, carry)

        lax.fori_loop(0, N_EXPERTS, fill_expert, 0)

        def fill_tail(i, carry):
            block_copy(pl.multiple_of(tail_ref[0] + i * tt, tt)).start()
            return carry

        lax.fori_loop(0, tail_ref[1], fill_tail, 0)

        def wait_row(r, carry):
            row_copy(0, 0).wait()
            return carry

        lax.fori_loop(0, tail_ref[2], wait_row, 0)

        def wait_block(i, carry):
            block_copy(0).wait()
            return carry

        lax.fori_loop(0, tail_ref[1], wait_block, 0)

    for _ in range(TOP_K):
        block_copy(0).wait()


def _dispatch(dest, h, counts, n_pad):
    t, d = h.shape
    tt = DISP_TT
    pc = (counts + MOE_SUB - 1) // MOE_SUB * MOE_SUB
    pstart = jnp.cumsum(pc) - pc
    total = jnp.sum(pc)
    pad_len = pc - counts
    tail = jnp.stack([total, (n_pad - total) // tt, jnp.sum(pad_len)]).astype(I32)
    grid_spec = pltpu.PrefetchScalarGridSpec(
        num_scalar_prefetch=3,
        grid=(t // tt,),
        in_specs=[pl.BlockSpec((TOP_K, tt), lambda i, *_: (0, i), memory_space=pltpu.SMEM),
                  pl.BlockSpec((tt, d), lambda i, *_: (i, 0))],
        out_specs=pl.BlockSpec(memory_space=pl.ANY),
        scratch_shapes=[pltpu.SemaphoreType.DMA(())],
    )
    return pl.pallas_call(
        _dispatch_kernel,
        out_shape=jax.ShapeDtypeStruct((n_pad, d), F32),
        grid_spec=grid_spec,
        compiler_params=_params("arbitrary"),
    )((pstart + counts).astype(I32), pad_len.astype(I32), tail, dest, h)


def _degate_kernel(w_ref, o_ref):
    blk = DEGATE_BLK
    r = lax.broadcasted_iota(I32, (blk, blk), 0)
    c = lax.broadcasted_iota(I32, (blk, blk), 1)
    src = jnp.where(c < blk // 2, 2 * c, 2 * (c - blk // 2) + 1)
    perm = jnp.where(r == src, 1.0, 0.0).astype(BF16)
    nblk = w_ref.shape[2] // blk
    for hb in range(nblk):
        res = _dot(w_ref[0, :, hb * blk:(hb + 1) * blk].astype(BF16), perm).astype(BF16)
        o_ref[0, :, hb * (blk // 2):(hb + 1) * (blk // 2)] = res[:, :blk // 2]
        o_ref[0, :, MOE_TF + hb * (blk // 2):MOE_TF + (hb + 1) * (blk // 2)] = res[:, blk // 2:]


def _degate(w_gu):
    e, d, n = w_gu.shape
    return pl.pallas_call(
        _degate_kernel,
        out_shape=jax.ShapeDtypeStruct((e, d, n), BF16),
        grid=(e, n // (2 * MOE_TF)),
        in_specs=[pl.BlockSpec((1, d, 2 * MOE_TF), lambda i, j: (i, 0, j))],
        out_specs=pl.BlockSpec((1, d, 2 * MOE_TF), lambda i, j: (i, 0, j)),
        compiler_params=_params("parallel", "parallel"),
    )(w_gu)


def _degate_bias(b_gu):
    e, n = b_gu.shape
    return b_gu.reshape(e, n // (2 * MOE_TF), MOE_TF, 2).transpose(0, 1, 3, 2).reshape(e, 1, n)


def _experts_kernel(tile_ref, exp_ref, lo_ref, hi_ref, first_ref,
                    xs_ref, wgl_ref, bgl_ref, wd_ref, bd_ref, y_ref):
    del tile_ref, exp_ref
    w = pl.program_id(0)
    j = pl.program_id(1)
    nj = pl.num_programs(1)
    lo = lo_ref[w]
    hi = hi_ref[w]

    @pl.when((j == 0) & (first_ref[w] == 1))
    def _():
        y_ref[...] = jnp.zeros_like(y_ref)

    for sblk in range(MOE_NSUB):
        @pl.when((lo <= sblk) & (sblk < hi))
        def _():
            r0 = sblk * MOE_SUB
            xb = xs_ref[r0:r0 + MOE_SUB, :].astype(BF16)
            gu = _dot(xb, wgl_ref[0]) + bgl_ref[0]
            glu = jnp.minimum(gu[:, :MOE_TF], SWIGLU_LIMIT)
            lin = jnp.clip(gu[:, MOE_TF:], -SWIGLU_LIMIT, SWIGLU_LIMIT)
            act = glu * jax.nn.sigmoid(SWIGLU_ALPHA * glu) * (lin + 1.0)
            y = _dot(act.astype(BF16), wd_ref[0])
            y = y + jnp.where(j == nj - 1, 1.0, 0.0) * bd_ref[0]
            y_ref[r0:r0 + MOE_SUB, :] += y


def _experts(items, xs, wgl, bgl, wd, bd):
    n_pad = xs.shape[0]
    n_items = items[0].shape[0]
    nj = D_EXPERT // MOE_TF
    grid_spec = pltpu.PrefetchScalarGridSpec(
        num_scalar_prefetch=5,
        grid=(n_items, nj),
        in_specs=[pl.BlockSpec((MOE_TM, D_MODEL), lambda w, j, tl, ex, lo, hi, fi: (tl[w], 0)),
                  pl.BlockSpec((1, D_MODEL, 2 * MOE_TF), lambda w, j, tl, ex, lo, hi, fi: (ex[w], 0, j)),
                  pl.BlockSpec((1, 1, 2 * MOE_TF), lambda w, j, tl, ex, lo, hi, fi: (ex[w], 0, j)),
                  pl.BlockSpec((1, MOE_TF, D_MODEL), lambda w, j, tl, ex, lo, hi, fi: (ex[w], j, 0)),
                  pl.BlockSpec((1, 1, D_MODEL), lambda w, j, tl, ex, lo, hi, fi: (ex[w], 0, 0))],
        out_specs=pl.BlockSpec((MOE_TM, D_MODEL), lambda w, j, tl, ex, lo, hi, fi: (tl[w], 0)),
    )
    return pl.pallas_call(
        _experts_kernel,
        out_shape=jax.ShapeDtypeStruct((n_pad, D_MODEL), F32),
        grid_spec=grid_spec,
        compiler_params=_params("arbitrary", "arbitrary"),
    )(*items, xs, wgl, bgl, wd, bd)


def _expert_items(counts, n_tiles):
    pc = (counts + MOE_SUB - 1) // MOE_SUB
    ends = jnp.cumsum(pc)
    starts = ends - pc
    t_lo = starts // MOE_NSUB
    t_hi = (ends + MOE_NSUB - 1) // MOE_NSUB
    n_it = jnp.where(pc > 0, t_hi - t_lo, 0)
    it_end = jnp.cumsum(n_it)
    it_start = it_end - n_it
    total = it_end[-1]
    n_items = n_tiles + N_EXPERTS - 1
    w = jnp.arange(n_items, dtype=I32)
    wc = jnp.minimum(w, total - 1)
    ex = jnp.minimum(jnp.sum((it_end[None, :] <= wc[:, None]).astype(I32), axis=1), N_EXPERTS - 1)
    tile = (t_lo[ex] + wc - it_start[ex]).astype(I32)
    live = w < total
    lo = jnp.where(live, jnp.maximum(starts[ex] - tile * MOE_NSUB, 0), 0).astype(I32)
    hi = jnp.where(live, jnp.minimum(ends[ex] - tile * MOE_NSUB, MOE_NSUB), 0).astype(I32)
    used = (ends[-1] + MOE_NSUB - 1) // MOE_NSUB
    spare = used + (w - total)
    tile = jnp.where(live, tile, jnp.minimum(spare, n_tiles - 1)).astype(I32)
    prev_tile = jnp.concatenate([jnp.full((1,), -1, I32), tile[:-1]])
    first = jnp.where(live, tile != prev_tile, spare < n_tiles).astype(I32)
    return tile, ex, lo, hi, first


def _combine_kernel(alpha, dcur_ref, dnxt_ref, ys_ref, h_ref, grow_ref, lg_ref, lb_ref, o_ref, buf, sem):
    i = pl.program_id(0)
    n = pl.num_programs(0)
    tt = COMB_TT
    slot = i % 2

    def issue(d_ref, s_):
        def body(t, carry):
            for k in range(TOP_K):
                pltpu.make_async_copy(ys_ref.at[pl.ds(d_ref[k, t], 1)], buf.at[s_, k, pl.ds(t, 1)], sem.at[s_]).start()
            return carry
        lax.fori_loop(0, tt, body, 0)

    @pl.when(i == 0)
    def _():
        issue(dcur_ref, 0)

    @pl.when(i + 1 < n)
    def _():
        issue(dnxt_ref, 1 - slot)

    for k in range(TOP_K):
        pltpu.make_async_copy(ys_ref.at[pl.ds(0, tt)], buf.at[slot, k], sem.at[slot]).wait()

    g = grow_ref[...]
    acc = g[:, 0:1] * buf[slot, 0]
    for k in range(1, TOP_K):
        acc = acc + g[:, k:k + 1] * buf[slot, k]
    o_ref[...] = _layer_norm(alpha * h_ref[...] + acc, lg_ref[...], lb_ref[...])


def _combine(alpha, dest, ys, h, grow, ln_g, ln_b):
    t, d = h.shape
    tt = COMB_TT
    n = t // tt
    return pl.pallas_call(
        functools.partial(_combine_kernel, alpha),
        out_shape=jax.ShapeDtypeStruct((t, d), F32),
        grid=(n,),
        in_specs=[pl.BlockSpec((TOP_K, tt), lambda i: (0, i), memory_space=pltpu.SMEM),
                  pl.BlockSpec((TOP_K, tt), lambda i: (0, jnp.minimum(i + 1, n - 1)), memory_space=pltpu.SMEM),
                  pl.BlockSpec(memory_space=pl.ANY),
                  pl.BlockSpec((tt, d), lambda i: (i, 0)),
                  pl.BlockSpec((tt, LANES), lambda i: (i, 0)),
                  pl.BlockSpec((1, d), lambda i: (0, 0)),
                  pl.BlockSpec((1, d), lambda i: (0, 0))],
        out_specs=pl.BlockSpec((tt, d), lambda i: (i, 0)),
        scratch_shapes=[pltpu.VMEM((2, TOP_K, tt, d), F32), pltpu.SemaphoreType.DMA((2,))],
        compiler_params=_params("arbitrary"),
    )(dest, dest, ys, h, grow, ln_g, ln_b)


def _regroup_w_in(w_in):
    gates = w_in[:, GATE_SRC:GATE_SRC + 3 * NSA_HEADS]
    per_head = 3 * NSA_GROUP
    gate_blocks = [jnp.pad(gates[:, h * per_head:(h + 1) * per_head], ((0, 0), (0, LANES - per_head)))
                   for h in range(NSA_KV_HEADS)]
    return jnp.concatenate([w_in[:, :GATE_SRC], w_in[:, GATE_SRC + 3 * NSA_HEADS:]] + gate_blocks, axis=1)


def _layer(depth, x, mem, w_in, cmp_pos_k, cmp_w1_k, cmp_w2_k, cmp_pos_v, cmp_w1_v, cmp_w2_v,
           pool_w, pool_scale, w_mem_kv, gn_nsa, gn_pool, gn_mem, w_out, ln1_g, ln1_b,
           w_router, b_router, w_gu, b_gu, w_dn, b_dn, ln2_g, ln2_b):
    b, s, d = x.shape
    t = b * s
    alpha = (2 * depth) ** 0.25
    row = lambda v: v.reshape(1, -1)

    proj = _matmul(x.reshape(t, d), _regroup_w_in(w_in), 1024, 768).reshape(b, s, PROJ_WIDTH)
    memkv = _matmul(mem.reshape(-1, d), w_mem_kv, 1024, 512).reshape(b, mem.shape[1], 2 * MEM_WIDTH)
    cmp_kv = _compress(proj, jnp.stack([cmp_pos_k, cmp_pos_v]), jnp.stack([cmp_w1_k, cmp_w1_v]),
                       jnp.stack([cmp_w2_k, cmp_w2_v]))
    onsa = _nsa(proj, cmp_kv)
    h, idx_t, _, grow = _mix(alpha, onsa, proj, memkv, pool_w, row(pool_scale), row(gn_nsa), row(gn_pool),
                             row(gn_mem), w_out.astype(BF16), x, row(ln1_g), row(ln1_b),
                             w_router.T, b_router.reshape(-1, 1))

    dest, cnt = _route(idx_t)
    counts = cnt[:, 0].astype(I32)
    n_pad = t * TOP_K + N_EXPERTS * MOE_SUB
    n_pad = (n_pad + MOE_TM - 1) // MOE_TM * MOE_TM
    xs = _dispatch(dest, h, counts, n_pad)
    items = _expert_items(counts, n_pad // MOE_TM)
    ys = _experts(items, xs, _degate(w_gu), _degate_bias(b_gu), w_dn.astype(BF16), b_dn[:, None, :])
    out = _combine(alpha, dest, ys, h, grow, row(ln2_g), row(ln2_b))
    return out.reshape(b, s, d)


def kernel(x, mem, w_in, cmp_pos_k, cmp_w1_k, cmp_w2_k, cmp_pos_v, cmp_w1_v, cmp_w2_v, pool_w, pool_scale,
           w_mem_kv, gn_nsa, gn_pool, gn_mem, w_out, ln1_g, ln1_b, w_router, b_router, w_gu, b_gu, w_dn, b_dn,
           ln2_g, ln2_b):
    depth = w_in.shape[0]
    for l in range(depth):
        x = _layer(depth, x, mem, w_in[l], cmp_pos_k[l], cmp_w1_k[l], cmp_w2_k[l], cmp_pos_v[l], cmp_w1_v[l],
                   cmp_w2_v[l], pool_w[l], pool_scale[l], w_mem_kv[l], gn_nsa[l], gn_pool[l], gn_mem[l],
                   w_out[l], ln1_g[l], ln1_b[l], w_router[l], b_router[l], w_gu[l], b_gu[l], w_dn[l], b_dn[l],
                   ln2_g[l], ln2_b[l])
    return x
```

```python
import functools

import jax
import jax.numpy as jnp
from jax import lax
from jax.experimental import pallas as pl
from jax.experimental.pallas import tpu as pltpu

D_MODEL = 2048
HEAD_DIM = 128
NSA_HEADS = 8
NSA_KV_HEADS = 2
NSA_GROUP = NSA_HEADS // NSA_KV_HEADS
NSA_WIDTH = NSA_HEADS * HEAD_DIM
NSA_KV_WIDTH = NSA_KV_HEADS * HEAD_DIM
CMP_LEN = 32
CMP_STRIDE = 16
CMP_HIDDEN = 2 * HEAD_DIM
SLC_LEN = 64
SLC_TOPK = 16
WINDOW = 512
POOL_WINDOWS = (2, 4, 8, 16)
POOL_GROUPS = 4
POOL_CH = 128
POOL_WIDTH = POOL_GROUPS * POOL_CH
MEM_HEADS = 4
MEM_HEAD_DIM = 128
MEM_WIDTH = MEM_HEADS * MEM_HEAD_DIM
N_EXPERTS = 32
TOP_K = 4
D_EXPERT = 2048
SWIGLU_ALPHA = 1.702
SWIGLU_LIMIT = 7.0
LN_EPS = 1e-5
NEG_INF = -1e30
LOG2E = 1.4426950408889634
BIG = 1e9

LANES = 128
SUBLANES = 8
VMEM_LIMIT = 56 * 1024 * 1024

COL_Q = 0
COL_KC = 1024
COL_KS = 1536
COL_KW = 2048
COL_POOL = 2560
COL_QMEM = 3072
COL_GATE = 3584
PROJ_WIDTH = 3840
GATE_SRC = 2560

NSA_TQ = 256
NSA_CK = 512
MIX_TT = 512
ROUTE_TT = 512
DISP_TT = 256
MOE_TM = 1024
MOE_SUB = 256
MOE_NSUB = MOE_TM // MOE_SUB
MOE_TF = 512
DEGATE_BLK = 512
assert MOE_SUB % DISP_TT == 0 and MOE_TM % MOE_SUB == 0
COMB_TT = 128

F32 = jnp.float32
BF16 = jnp.bfloat16
U32 = jnp.uint32
I32 = jnp.int32


def _dot(a, b):
    return jnp.dot(a, b, preferred_element_type=F32)


def _dot_nt(a, b):
    return lax.dot_general(a, b, (((1,), (1,)), ((), ())), preferred_element_type=F32)


def _params(*sem):
    return pltpu.CompilerParams(dimension_semantics=sem, vmem_limit_bytes=VMEM_LIMIT)


def _mm_kernel(x_ref, w_ref, o_ref, xb_ref):
    @pl.when(pl.program_id(1) == 0)
    def _():
        xb_ref[...] = x_ref[...].astype(BF16)

    o_ref[...] = _dot(xb_ref[...], w_ref[...].astype(BF16))


def _matmul(x, w, tm, tn):
    m, k = x.shape
    n = w.shape[1]
    return pl.pallas_call(
        _mm_kernel,
        out_shape=jax.ShapeDtypeStruct((m, n), F32),
        grid=(m // tm, n // tn),
        in_specs=[pl.BlockSpec((tm, k), lambda i, j: (i, 0)),
                  pl.BlockSpec((k, tn), lambda i, j: (0, j))],
        out_specs=pl.BlockSpec((tm, tn), lambda i, j: (i, j)),
        scratch_shapes=[pltpu.VMEM((tm, k), BF16)],
        compiler_params=_params("parallel", "arbitrary"),
    )(x, w)


def _gelu_tanh(x):
    return 0.5 * x * (1.0 + jnp.tanh(0.7978845608028654 * (x + 0.044715 * (x * x * x))))


def _cmp_kernel(kv_ref, pos_ref, w1_ref, w2_ref, o_ref):
    nchunk = kv_ref.shape[1] // CMP_STRIDE
    top = jnp.zeros((nchunk, CMP_HIDDEN), F32)
    bot = jnp.zeros((nchunk, CMP_HIDDEN), F32)
    for l in range(CMP_STRIDE):
        rows = kv_ref[0, pl.ds(l, nchunk, stride=CMP_STRIDE), :]
        a = (rows + pos_ref[0, l:l + 1, :]).astype(BF16)
        b = (rows + pos_ref[0, CMP_STRIDE + l:CMP_STRIDE + l + 1, :]).astype(BF16)
        top = top + _dot(a, w1_ref[0, l * HEAD_DIM:(l + 1) * HEAD_DIM, :].astype(BF16))
        bot = bot + _dot(b, w1_ref[0, (CMP_STRIDE + l) * HEAD_DIM:(CMP_STRIDE + l + 1) * HEAD_DIM, :].astype(BF16))
    hidden = top + pltpu.roll(bot, nchunk - 1, 0)
    act = _gelu_tanh(hidden)
    o_ref[0, 0] = _dot(act.astype(BF16), w2_ref[0].astype(BF16))


def _compress(proj, pos, w1, w2):
    b, s, _ = proj.shape
    nchunk = s // CMP_STRIDE
    ncol = 2 * NSA_KV_HEADS
    col0 = COL_KC // HEAD_DIM
    return pl.pallas_call(
        _cmp_kernel,
        out_shape=jax.ShapeDtypeStruct((b, ncol, nchunk, HEAD_DIM), F32),
        grid=(b, ncol),
        in_specs=[pl.BlockSpec((1, s, HEAD_DIM), lambda i, c: (i, 0, col0 + c)),
                  pl.BlockSpec((1, CMP_LEN, HEAD_DIM), lambda i, c: (c // NSA_KV_HEADS, 0, 0)),
                  pl.BlockSpec((1, CMP_LEN * HEAD_DIM, CMP_HIDDEN), lambda i, c: (c // NSA_KV_HEADS, 0, 0)),
                  pl.BlockSpec((1, CMP_HIDDEN, HEAD_DIM), lambda i, c: (c // NSA_KV_HEADS, 0, 0))],
        out_specs=pl.BlockSpec((1, 1, nchunk, HEAD_DIM), lambda i, c: (i, c, 0, 0)),
        compiler_params=_params("parallel", "arbitrary"),
    )(proj, pos, w1, w2)


def _split3(x):
    hi = x.astype(BF16)
    r = x - hi.astype(F32)
    mid = r.astype(BF16)
    lo = (r - mid.astype(F32)).astype(BF16)
    return hi, mid, lo


def _nsa_kernel(q_ref, g_ref, kc_ref, vc_ref, ks_ref, vs_ref, kw_ref, vw_ref, o_ref,
                ksb, vsb, kwb, vwb, exp_ref):
    qi = pl.program_id(2)
    tq = NSA_TQ
    grp = NSA_GROUP
    rows = grp * tq
    seq = ks_ref.shape[1]
    n_c = kc_ref.shape[2]
    n_s = seq // SLC_LEN

    @pl.when(qi == 0)
    def _():
        ksb[...] = ks_ref[0].astype(BF16)
        vsb[...] = vs_ref[0].astype(BF16)
        kwb[...] = kw_ref[0].astype(BF16)
        vwb[...] = vw_ref[0].astype(BF16)
        blk = lax.broadcasted_iota(I32, (LANES, seq), 0)
        key_blk = lax.broadcasted_iota(I32, (LANES, seq), 1) // SLC_LEN
        exp_ref[...] = jnp.where(blk == key_blk, 1.0, 0.0).astype(BF16)

    t0 = qi * tq
    q = q_ref[0] * (HEAD_DIM ** -0.5 * LOG2E)
    qb = jnp.concatenate([q[:, g * HEAD_DIM:(g + 1) * HEAD_DIM] for g in range(grp)], axis=0).astype(BF16)
    tok = t0 + lax.broadcasted_iota(I32, (tq, 1), 0)

    span = WINDOW + tq
    wstart = pl.multiple_of(jnp.maximum(t0 - WINDOW, 0), tq)
    kb = kwb[pl.ds(wstart, span), :]
    vb = vwb[pl.ds(wstart, span), :]
    key = wstart + lax.broadcasted_iota(I32, (1, span), 1)
    bias_w = jnp.where((key <= tok) & (key > tok - WINDOW), 0.0, NEG_INF)
    sw = _dot_nt(qb, kb).reshape(grp, tq, span) + bias_w[None]
    pw = jnp.exp2(sw - jnp.max(sw, axis=-1, keepdims=True))
    l_w = jnp.sum(pw, axis=-1, keepdims=True)
    o_win = _dot(pw.reshape(rows, span).astype(BF16), vb).reshape(grp, tq, HEAD_DIM) / l_w

    kc = kc_ref[0, 0].astype(BF16)
    vc = vc_ref[0, 0].astype(BF16)
    cidx = lax.broadcasted_iota(I32, (1, n_c), 1)
    mask_c = (cidx * CMP_STRIDE + (CMP_LEN - 1) <= tok) & (cidx < n_c - 1)
    s = _dot_nt(qb, kc).reshape(grp, tq, n_c) + jnp.where(mask_c, 0.0, NEG_INF)[None]
    m = jnp.max(s, axis=-1, keepdims=True)
    any_c = jnp.where(tok >= CMP_LEN - 1, 1.0, 0.0)
    e = jnp.exp2(s - m) * any_c[None]
    den = jnp.sum(e, axis=-1, keepdims=True)
    p_c = e / jnp.where(den > 0.0, den, 1.0)
    o_cmp = _dot(p_c.reshape(rows, n_c).astype(BF16), vc)

    psum = p_c[0]
    for g in range(1, grp):
        psum = psum + p_c[g]
    jcol = lax.broadcasted_iota(I32, (n_s, 1), 0)
    crow = lax.broadcasted_iota(I32, (1, n_c), 1)
    ov = jnp.minimum(crow * CMP_STRIDE + CMP_LEN, jcol * SLC_LEN + SLC_LEN) - jnp.maximum(crow * CMP_STRIDE, jcol * SLC_LEN)
    ov = jnp.where(crow < n_c - 1, jnp.maximum(ov, 0) // CMP_STRIDE, 0).astype(F32).astype(BF16)
    hi, mid, lo = _split3(psum)
    imp = _dot_nt(ov, hi) + _dot_nt(ov, mid) + _dot_nt(ov, lo)
    tok_l = t0 + lax.broadcasted_iota(I32, (1, tq), 1)
    cur = tok_l // SLC_LEN
    forced = (jcol == 0) | (jcol == cur) | (jcol == cur - 1)
    imp = jnp.where(forced, BIG, jnp.where(jcol <= cur, imp, -BIG))
    rank = jnp.zeros((n_s, tq), F32)
    for i in range(n_s):
        row = imp[i:i + 1, :]
        beats = (row > imp) | ((row == imp) & (jcol > i))
        rank = rank + jnp.where(beats, 1.0, 0.0)
    sel_t = jnp.where(rank < float(min(SLC_TOPK, n_s)), 1.0, 0.0)
    sel_t = jnp.concatenate([sel_t, jnp.zeros((LANES - n_s, tq), F32)], axis=0)
    sel = sel_t.T.astype(BF16)

    ck = NSA_CK

    def slc_body(c, carry):
        m_i, l_i, acc = carry
        start = pl.multiple_of(c * ck, ck)
        kb = ksb[pl.ds(start, ck), :]
        vb = vsb[pl.ds(start, ck), :]
        key = c * ck + lax.broadcasted_iota(I32, (1, ck), 1)
        picked = _dot(sel, exp_ref[:, pl.ds(start, ck)])
        bias = jnp.where((picked > 0.5) & (key <= tok), 0.0, NEG_INF)
        sc = _dot_nt(qb, kb).reshape(grp, tq, ck) + bias[None]
        m_new = jnp.maximum(m_i, jnp.max(sc, axis=-1, keepdims=True))
        alpha = jnp.exp2(m_i - m_new)
        p = jnp.exp2(sc - m_new)
        l_new = alpha * l_i + jnp.sum(p, axis=-1, keepdims=True)
        pv = _dot(p.reshape(rows, ck).astype(BF16), vb).reshape(grp, tq, HEAD_DIM)
        return m_new, l_new, alpha * acc + pv

    n_chunks = (t0 + tq + ck - 1) // ck
    m0 = jnp.full((grp, tq, 1), NEG_INF, F32)
    l0 = jnp.zeros((grp, tq, 1), F32)
    a0 = jnp.zeros((grp, tq, HEAD_DIM), F32)
    _, l_s, acc_s = lax.fori_loop(0, n_chunks, slc_body, (m0, l0, a0))
    o_slc = acc_s / l_s

    gate = jax.nn.sigmoid(g_ref[0])
    o_cmp = o_cmp.reshape(grp, tq, HEAD_DIM)
    for g in range(grp):
        o = (gate[:, 3 * g:3 * g + 1] * o_cmp[g] + gate[:, 3 * g + 1:3 * g + 2] * o_slc[g]
             + gate[:, 3 * g + 2:3 * g + 3] * o_win[g])
        o_ref[0, :, g * HEAD_DIM:(g + 1) * HEAD_DIM] = o


def _nsa(proj, cmp_kv):
    b, s, _ = proj.shape
    tq = NSA_TQ
    gw = NSA_GROUP * HEAD_DIM
    n_c = cmp_kv.shape[2]

    def col(c0):
        return lambda i, h, j: (i, 0, c0 // HEAD_DIM + h)

    kv_spec = lambda c0: pl.BlockSpec((1, s, HEAD_DIM), col(c0))
    return pl.pallas_call(
        _nsa_kernel,
        out_shape=jax.ShapeDtypeStruct((b, s, NSA_WIDTH), F32),
        grid=(b, NSA_KV_HEADS, s // tq),
        in_specs=[pl.BlockSpec((1, tq, gw), lambda i, h, j: (i, j, h)),
                  pl.BlockSpec((1, tq, LANES), lambda i, h, j: (i, j, COL_GATE // LANES + h)),
                  pl.BlockSpec((1, 1, n_c, HEAD_DIM), lambda i, h, j: (i, h, 0, 0)),
                  pl.BlockSpec((1, 1, n_c, HEAD_DIM), lambda i, h, j: (i, NSA_KV_HEADS + h, 0, 0)),
                  kv_spec(COL_KS), kv_spec(COL_KS + NSA_KV_WIDTH),
                  kv_spec(COL_KW), kv_spec(COL_KW + NSA_KV_WIDTH)],
        out_specs=pl.BlockSpec((1, tq, gw), lambda i, h, j: (i, j, h)),
        scratch_shapes=[pltpu.VMEM((s, HEAD_DIM), BF16) for _ in range(4)] + [pltpu.VMEM((LANES, s), BF16)],
        compiler_params=_params("parallel", "parallel", "arbitrary"),
    )(proj, proj, cmp_kv, cmp_kv, proj, proj, proj, proj)


def _rms(x, g):
    return x * lax.rsqrt(jnp.mean(x * x, axis=-1, keepdims=True) + LN_EPS) * g


def _layer_norm(x, g, b):
    mu = jnp.mean(x, axis=-1, keepdims=True)
    xc = x - mu
    var = jnp.mean(xc * xc, axis=-1, keepdims=True)
    return xc * lax.rsqrt(var + LN_EPS) * g + b


def _mix_kernel(alpha, onsa_ref, u_ref, halo_ref, qm_ref, mkv_ref, pw_ref, ps_ref, gnn_ref, gnp_ref, gnm_ref,
                wout_ref, x_ref, lg_ref, lb_ref, wr_ref, br_ref,
                h_ref, idx_ref, gt_ref, grow_ref):
    i = pl.program_id(1)
    tt = MIX_TT
    halo_n = halo_ref.shape[1]

    u = u_ref[0]
    halo = jnp.where(i > 0, halo_ref[0], 0.0)
    ext = jnp.concatenate([halo, u], axis=0)
    tok = i * tt + lax.broadcasted_iota(I32, (tt, 1), 0)
    pooled = []
    for g, win in enumerate(POOL_WINDOWS):
        w = ext[:, g * POOL_CH:(g + 1) * POOL_CH]
        step = 1
        while step < win:
            w = w + pltpu.roll(w, step, 0)
            step *= 2
        cnt = jnp.minimum(tok + 1, win).astype(F32)
        ug = u[:, g * POOL_CH:(g + 1) * POOL_CH]
        pg = (w[halo_n:, :] / cnt - ug).astype(BF16)
        yg = _dot(pg, pw_ref[g].astype(BF16)) * ps_ref[:, g * POOL_CH:(g + 1) * POOL_CH]
        pooled.append(yg)
    o_pool = jnp.concatenate(pooled, axis=-1)

    qm = qm_ref[0]
    mkv = mkv_ref[0]
    heads = []
    for hh in range(MEM_HEADS):
        sl = slice(hh * MEM_HEAD_DIM, (hh + 1) * MEM_HEAD_DIM)
        kh = mkv[:, sl].astype(BF16)
        vh = mkv[:, MEM_WIDTH + hh * MEM_HEAD_DIM:MEM_WIDTH + (hh + 1) * MEM_HEAD_DIM].astype(BF16)
        sc = _dot_nt(qm[:, sl].astype(BF16), kh) * (MEM_HEAD_DIM ** -0.5)
        sc = sc - jnp.max(sc, axis=-1, keepdims=True)
        p = jnp.exp(sc)
        p = p / jnp.sum(p, axis=-1, keepdims=True)
        heads.append(_dot(p.astype(BF16), vh))
    o_mem = jnp.concatenate(heads, axis=-1)

    cat = jnp.concatenate([_rms(onsa_ref[0], gnn_ref[...]), _rms(o_pool, gnp_ref[...]),
                           _rms(o_mem, gnm_ref[...])], axis=-1).astype(BF16)
    mixed = _dot(cat, wout_ref[...])
    h = _layer_norm(alpha * x_ref[0] + mixed, lg_ref[...], lb_ref[...])
    h_ref[...] = h

    h_hi = h.astype(BF16)
    h_lo = (h - h_hi.astype(F32)).astype(BF16)
    wr = wr_ref[...]
    w_hi = wr.astype(BF16)
    w_lo = (wr - w_hi.astype(F32)).astype(BF16)
    logits = _dot_nt(w_hi, h_hi) + _dot_nt(w_hi, h_lo) + _dot_nt(w_lo, h_hi) + br_ref[...]
    ecol = lax.broadcasted_iota(I32, (N_EXPERTS, 1), 0)
    vals, idxs = [], []
    work = logits
    for _ in range(TOP_K):
        mx = jnp.max(work, axis=0, keepdims=True)
        ix = jnp.min(jnp.where(work == mx, ecol, N_EXPERTS), axis=0, keepdims=True)
        vals.append(mx)
        idxs.append(ix)
        work = jnp.where(ecol == ix, -jnp.inf, work)
    ex = [jnp.exp(v - vals[0]) for v in vals]
    den = ex[0] + ex[1] + ex[2] + ex[3]
    gates = jnp.concatenate([e_ / den for e_ in ex], axis=0)
    idx_ref[...] = jnp.concatenate(idxs, axis=0)
    gt_ref[...] = gates
    gpad = jnp.concatenate([gates, jnp.zeros((LANES - TOP_K, tt), F32)], axis=0)
    grow_ref[...] = gpad.T


def _mix(alpha, onsa, proj, memkv, pool_w, pool_scale, gn_nsa, gn_pool, gn_mem, wout_b, x, ln_g, ln_b, wr_t, br):
    b, s, d = x.shape
    t = b * s
    tt = MIX_TT
    nt = s // tt
    halo = 16
    vec = lambda n: pl.BlockSpec((1, n), lambda i, j: (0, 0))
    return pl.pallas_call(
        functools.partial(_mix_kernel, alpha),
        out_shape=(jax.ShapeDtypeStruct((t, d), F32),
                   jax.ShapeDtypeStruct((TOP_K, t), I32),
                   jax.ShapeDtypeStruct((TOP_K, t), F32),
                   jax.ShapeDtypeStruct((t, LANES), F32)),
        grid=(b, nt),
        in_specs=[pl.BlockSpec((1, tt, NSA_WIDTH), lambda i, j: (i, j, 0)),
                  pl.BlockSpec((1, tt, POOL_WIDTH), lambda i, j: (i, j, COL_POOL // POOL_WIDTH)),
                  pl.BlockSpec((1, halo, POOL_WIDTH),
                               lambda i, j: (i, jnp.maximum(j * (tt // halo) - 1, 0), COL_POOL // POOL_WIDTH)),
                  pl.BlockSpec((1, tt, MEM_WIDTH), lambda i, j: (i, j, COL_QMEM // MEM_WIDTH)),
                  pl.BlockSpec((1, memkv.shape[1], 2 * MEM_WIDTH), lambda i, j: (i, 0, 0)),
                  pl.BlockSpec((POOL_GROUPS, POOL_CH, POOL_CH), lambda i, j: (0, 0, 0)),
                  vec(POOL_WIDTH), vec(NSA_WIDTH), vec(POOL_WIDTH), vec(MEM_WIDTH),
                  pl.BlockSpec((d, d), lambda i, j: (0, 0)),
                  pl.BlockSpec((1, tt, d), lambda i, j: (i, j, 0)),
                  vec(d), vec(d),
                  pl.BlockSpec((N_EXPERTS, d), lambda i, j: (0, 0)),
                  pl.BlockSpec((N_EXPERTS, 1), lambda i, j: (0, 0))],
        out_specs=(pl.BlockSpec((tt, d), lambda i, j: (i * nt + j, 0)),
                   pl.BlockSpec((TOP_K, tt), lambda i, j: (0, i * nt + j)),
                   pl.BlockSpec((TOP_K, tt), lambda i, j: (0, i * nt + j)),
                   pl.BlockSpec((tt, LANES), lambda i, j: (i * nt + j, 0))),
        compiler_params=_params("parallel", "arbitrary"),
    )(onsa, proj, proj, proj, memkv, pool_w, pool_scale, gn_nsa, gn_pool, gn_mem, wout_b, x, ln_g, ln_b, wr_t, br)


def _route_kernel(idx_ref, dest_ref, cnt_ref, run_ref, base_ref):
    ph = pl.program_id(0)
    i = pl.program_id(1)
    n = pl.num_programs(1)
    tt = ROUTE_TT
    ecol = lax.broadcasted_iota(I32, (N_EXPERTS, 1), 0)
    idx = idx_ref[...]
    onehot = jnp.zeros((N_EXPERTS, tt), F32)
    for k in range(TOP_K):
        onehot = onehot + jnp.where(idx[k:k + 1, :] == ecol, 1.0, 0.0)

    @pl.when(i == 0)
    def _():
        run_ref[...] = jnp.zeros_like(run_ref)

    @pl.when(ph == 0)
    def _():
        run_ref[...] = run_ref[...] + jnp.sum(onehot, axis=1, keepdims=True)

        @pl.when(i == n - 1)
        def _():
            cnt = run_ref[...]
            cnt_ref[...] = cnt
            pc = jnp.floor((cnt + (MOE_SUB - 1)) * (1.0 / MOE_SUB)) * MOE_SUB
            inc = pc
            d = 1
            while d < N_EXPERTS:
                inc = inc + jnp.where(ecol >= d, pltpu.roll(inc, d, 0), 0.0)
                d *= 2
            base_ref[...] = inc - pc

    @pl.when(ph == 1)
    def _():
        r = lax.broadcasted_iota(I32, (tt, tt), 0)
        c = lax.broadcasted_iota(I32, (tt, tt), 1)
        before = jnp.where(r < c, 1.0, 0.0).astype(BF16)
        pref = _dot(onehot.astype(BF16), before)
        pos = pref + run_ref[:, 0:1] + base_ref[:, 0:1]
        rows_ = [jnp.sum(jnp.where(idx[k:k + 1, :] == ecol, pos, 0.0), axis=0, keepdims=True) for k in range(TOP_K)]
        dest_ref[...] = jnp.concatenate(rows_, axis=0).astype(I32)
        run_ref[...] = run_ref[...] + jnp.sum(onehot, axis=1, keepdims=True)


def _route(idx_t):
    t = idx_t.shape[1]
    tt = ROUTE_TT
    return pl.pallas_call(
        _route_kernel,
        out_shape=(jax.ShapeDtypeStruct((TOP_K, t), I32),
                   jax.ShapeDtypeStruct((N_EXPERTS, LANES), F32)),
        grid=(2, t // tt),
        in_specs=[pl.BlockSpec((TOP_K, tt), lambda p, i: (0, i))],
        out_specs=(pl.BlockSpec((TOP_K, tt), lambda p, i: (0, i * p)),
                   pl.BlockSpec((N_EXPERTS, LANES), lambda p, i: (0, 0))),
        scratch_shapes=[pltpu.VMEM((N_EXPERTS, LANES), F32), pltpu.VMEM((N_EXPERTS, LANES), F32)],
        compiler_params=_params("arbitrary", "arbitrary"),
    )(idx_t)


def _dispatch_kernel(pad_start_ref, pad_len_ref, tail_ref, dest_ref, h_ref, xs_ref, sem):
    tt = DISP_TT

    def row_copy(src_row, dst_row):
        return pltpu.make_async_copy(h_ref.at[pl.ds(src_row, 1)], xs_ref.at[pl.ds(dst_row, 1)], sem)

    def block_copy(dst_row):
        return pltpu.make_async_copy(h_ref, xs_ref.at[pl.ds(dst_row, tt)], sem)

    def issue(t, carry):
        for k in range(TOP_K):
            row_copy(t, dest_ref[k, t]).start(priority=k % 2)
        return carry

    lax.fori_loop(0, tt, issue, 0)

    @pl.when(pl.program_id(0) == 0)
    def _():
        def fill_expert(e, carry):
            def fill(r, c):
                row_copy(0, pad_start_ref[e] + r).start()
                return c
            return lax.fori_loop(0, pad_len_ref[e], fill, carry)

        lax.fori_loop(0, N_EXPERTS, fill_expert, 0)

        def fill_tail(i, carry):
            block_copy(pl.multiple_of(tail_ref[0] + i * tt, tt)).start()
            return carry

        lax.fori_loop(0, tail_ref[1], fill_tail, 0)

        def wait_row(r, carry):
            row_copy(0, 0).wait()
            return carry

        lax.fori_loop(0, tail_ref[2], wait_row, 0)

        def wait_block(i, carry):
            block_copy(0).wait()
            return carry

        lax.fori_loop(0, tail_ref[1], wait_block, 0)

    for _ in range(TOP_K):
        block_copy(0).wait()


def _dispatch(dest, h, counts, n_pad):
    t, d = h.shape
    tt = DISP_TT
    pc = (counts + MOE_SUB - 1) // MOE_SUB * MOE_SUB
    pstart = jnp.cumsum(pc) - pc
    total = jnp.sum(pc)
    pad_len = pc - counts
    tail = jnp.stack([total, (n_pad - total) // tt, jnp.sum(pad_len)]).astype(I32)
    grid_spec = pltpu.PrefetchScalarGridSpec(
        num_scalar_prefetch=3,
        grid=(t // tt,),
        in_specs=[pl.BlockSpec((TOP_K, tt), lambda i, *_: (0, i), memory_space=pltpu.SMEM),
                  pl.BlockSpec((tt, d), lambda i, *_: (i, 0))],
        out_specs=pl.BlockSpec(memory_space=pl.ANY),
        scratch_shapes=[pltpu.SemaphoreType.DMA(())],
    )
    return pl.pallas_call(
        _dispatch_kernel,
        out_shape=jax.ShapeDtypeStruct((n_pad, d), F32),
        grid_spec=grid_spec,
        compiler_params=_params("arbitrary"),
    )((pstart + counts).astype(I32), pad_len.astype(I32), tail, dest, h)


def _degate_kernel(w_ref, o_ref):
    blk = DEGATE_BLK
    r = lax.broadcasted_iota(I32, (blk, blk), 0)
    c = lax.broadcasted_iota(I32, (blk, blk), 1)
    src = jnp.where(c < blk // 2, 2 * c, 2 * (c - blk // 2) + 1)
    perm = jnp.where(r == src, 1.0, 0.0).astype(BF16)
    nblk = w_ref.shape[2] // blk
    for hb in range(nblk):
        res = _dot(w_ref[0, :, hb * blk:(hb + 1) * blk].astype(BF16), perm).astype(BF16)
        o_ref[0, :, hb * (blk // 2):(hb + 1) * (blk // 2)] = res[:, :blk // 2]
        o_ref[0, :, MOE_TF + hb * (blk // 2):MOE_TF + (hb + 1) * (blk // 2)] = res[:, blk // 2:]


def _degate(w_gu):
    e, d, n = w_gu.shape
    return pl.pallas_call(
        _degate_kernel,
        out_shape=jax.ShapeDtypeStruct((e, d, n), BF16),
        grid=(e, n // (2 * MOE_TF)),
        in_specs=[pl.BlockSpec((1, d, 2 * MOE_TF), lambda i, j: (i, 0, j))],
        out_specs=pl.BlockSpec((1, d, 2 * MOE_TF), lambda i, j: (i, 0, j)),
        compiler_params=_params("parallel", "parallel"),
    )(w_gu)


def _degate_bias(b_gu):
    e, n = b_gu.shape
    return b_gu.reshape(e, n // (2 * MOE_TF), MOE_TF, 2).transpose(0, 1, 3, 2).reshape(e, 1, n)


def _experts_kernel(tile_ref, exp_ref, lo_ref, hi_ref, first_ref,
                    xs_ref, wgl_ref, bgl_ref, wd_ref, bd_ref, y_ref):
    del tile_ref, exp_ref
    w = pl.program_id(0)
    j = pl.program_id(1)
    nj = pl.num_programs(1)
    lo = lo_ref[w]
    hi = hi_ref[w]

    @pl.when((j == 0) & (first_ref[w] == 1))
    def _():
        y_ref[...] = jnp.zeros_like(y_ref)

    last = jnp.where(j == nj - 1, 1.0, 0.0)

    def ffn_rows(r0, nrows):
        xb = xs_ref[r0:r0 + nrows, :].astype(BF16)
        gu = _dot(xb, wgl_ref[0]) + bgl_ref[0]
        glu = jnp.minimum(gu[:, :MOE_TF], SWIGLU_LIMIT)
        lin = jnp.clip(gu[:, MOE_TF:], -SWIGLU_LIMIT, SWIGLU_LIMIT)
        act = glu * jax.nn.sigmoid(SWIGLU_ALPHA * glu) * (lin + 1.0)
        y = _dot(act.astype(BF16), wd_ref[0]) + last * bd_ref[0]
        y_ref[r0:r0 + nrows, :] += y

    full = (lo == 0) & (hi == MOE_NSUB)

    @pl.when(full)
    def _():
        ffn_rows(0, MOE_TM)

    for pair in range(MOE_NSUB // 2):
        a, b = 2 * pair, 2 * pair + 1
        both = jnp.logical_not(full) & (lo <= a) & (b < hi)

        @pl.when(both)
        def _():
            ffn_rows(a * MOE_SUB, 2 * MOE_SUB)

        for sblk in (a, b):
            @pl.when(jnp.logical_not(full) & jnp.logical_not(both) & (lo <= sblk) & (sblk < hi))
            def _():
                ffn_rows(sblk * MOE_SUB, MOE_SUB)


def _experts(items, xs, wgl, bgl, wd, bd):
    n_pad = xs.shape[0]
    n_items = items[0].shape[0]
    nj = D_EXPERT // MOE_TF
    grid_spec = pltpu.PrefetchScalarGridSpec(
        num_scalar_prefetch=5,
        grid=(n_items, nj),
        in_specs=[pl.BlockSpec((MOE_TM, D_MODEL), lambda w, j, tl, ex, lo, hi, fi: (tl[w], 0)),
                  pl.BlockSpec((1, D_MODEL, 2 * MOE_TF), lambda w, j, tl, ex, lo, hi, fi: (ex[w], 0, j)),
                  pl.BlockSpec((1, 1, 2 * MOE_TF), lambda w, j, tl, ex, lo, hi, fi: (ex[w], 0, j)),
                  pl.BlockSpec((1, MOE_TF, D_MODEL), lambda w, j, tl, ex, lo, hi, fi: (ex[w], j, 0)),
                  pl.BlockSpec((1, 1, D_MODEL), lambda w, j, tl, ex, lo, hi, fi: (ex[w], 0, 0))],
        out_specs=pl.BlockSpec((MOE_TM, D_MODEL), lambda w, j, tl, ex, lo, hi, fi: (tl[w], 0)),
    )
    return pl.pallas_call(
        _experts_kernel,
        out_shape=jax.ShapeDtypeStruct((n_pad, D_MODEL), F32),
        grid_spec=grid_spec,
        compiler_params=_params("arbitrary", "arbitrary"),
    )(*items, xs, wgl, bgl, wd, bd)


def _expert_items(counts, n_tiles):
    pc = (counts + MOE_SUB - 1) // MOE_SUB
    ends = jnp.cumsum(pc)
    starts = ends - pc
    t_lo = starts // MOE_NSUB
    t_hi = (ends + MOE_NSUB - 1) // MOE_NSUB
    n_it = jnp.where(pc > 0, t_hi - t_lo, 0)
    it_end = jnp.cumsum(n_it)
    it_start = it_end - n_it
    total = it_end[-1]
    n_items = n_tiles + N_EXPERTS - 1
    w = jnp.arange(n_items, dtype=I32)
    wc = jnp.minimum(w, total - 1)
    ex = jnp.minimum(jnp.sum((it_end[None, :] <= wc[:, None]).astype(I32), axis=1), N_EXPERTS - 1)
    tile = (t_lo[ex] + wc - it_start[ex]).astype(I32)
    live = w < total
    lo = jnp.where(live, jnp.maximum(starts[ex] - tile * MOE_NSUB, 0), 0).astype(I32)
    hi = jnp.where(live, jnp.minimum(ends[ex] - tile * MOE_NSUB, MOE_NSUB), 0).astype(I32)
    used = (ends[-1] + MOE_NSUB - 1) // MOE_NSUB
    spare = used + (w - total)
    tile = jnp.where(live, tile, jnp.minimum(spare, n_tiles - 1)).astype(I32)
    prev_tile = jnp.concatenate([jnp.full((1,), -1, I32), tile[:-1]])
    first = jnp.where(live, tile != prev_tile, spare < n_tiles).astype(I32)
    return tile, ex, lo, hi, first


def _combine_kernel(alpha, dcur_ref, dnxt_ref, ys_ref, h_ref, grow_ref, lg_ref, lb_ref, o_ref, buf, sem):
    i = pl.program_id(0)
    n = pl.num_programs(0)
    tt = COMB_TT
    slot = i % 2

    def issue(d_ref, s_):
        def body(t, carry):
            for k in range(TOP_K):
                pltpu.make_async_copy(ys_ref.at[pl.ds(d_ref[k, t], 1)], buf.at[s_, k, pl.ds(t, 1)],
                                      sem.at[s_]).start(priority=k % 2)
            return carry
        lax.fori_loop(0, tt, body, 0)

    @pl.when(i == 0)
    def _():
        issue(dcur_ref, 0)

    @pl.when(i + 1 < n)
    def _():
        issue(dnxt_ref, 1 - slot)

    for k in range(TOP_K):
        pltpu.make_async_copy(ys_ref.at[pl.ds(0, tt)], buf.at[slot, k], sem.at[slot]).wait()

    g = grow_ref[...]
    acc = g[:, 0:1] * buf[slot, 0]
    for k in range(1, TOP_K):
        acc = acc + g[:, k:k + 1] * buf[slot, k]
    o_ref[...] = _layer_norm(alpha * h_ref[...] + acc, lg_ref[...], lb_ref[...])


def _combine(alpha, dest, ys, h, grow, ln_g, ln_b):
    t, d = h.shape
    tt = COMB_TT
    n = t // tt
    return pl.pallas_call(
        functools.partial(_combine_kernel, alpha),
        out_shape=jax.ShapeDtypeStruct((t, d), F32),
        grid=(n,),
        in_specs=[pl.BlockSpec((TOP_K, tt), lambda i: (0, i), memory_space=pltpu.SMEM),
                  pl.BlockSpec((TOP_K, tt), lambda i: (0, jnp.minimum(i + 1, n - 1)), memory_space=pltpu.SMEM),
                  pl.BlockSpec(memory_space=pl.ANY),
                  pl.BlockSpec((tt, d), lambda i: (i, 0)),
                  pl.BlockSpec((tt, LANES), lambda i: (i, 0)),
                  pl.BlockSpec((1, d), lambda i: (0, 0)),
                  pl.BlockSpec((1, d), lambda i: (0, 0))],
        out_specs=pl.BlockSpec((tt, d), lambda i: (i, 0)),
        scratch_shapes=[pltpu.VMEM((2, TOP_K, tt, d), F32), pltpu.SemaphoreType.DMA((2,))],
        compiler_params=_params("arbitrary"),
    )(dest, dest, ys, h, grow, ln_g, ln_b)


def _regroup_w_in(w_in):
    gates = w_in[:, GATE_SRC:GATE_SRC + 3 * NSA_HEADS]
    per_head = 3 * NSA_GROUP
    gate_blocks = [jnp.pad(gates[:, h * per_head:(h + 1) * per_head], ((0, 0), (0, LANES - per_head)))
                   for h in range(NSA_KV_HEADS)]
    return jnp.concatenate([w_in[:, :GATE_SRC], w_in[:, GATE_SRC + 3 * NSA_HEADS:]] + gate_blocks, axis=1)


def _layer(depth, x, mem, w_in, cmp_pos_k, cmp_w1_k, cmp_w2_k, cmp_pos_v, cmp_w1_v, cmp_w2_v,
           pool_w, pool_scale, w_mem_kv, gn_nsa, gn_pool, gn_mem, w_out, ln1_g, ln1_b,
           w_router, b_router, w_gu, b_gu, w_dn, b_dn, ln2_g, ln2_b):
    b, s, d = x.shape
    t = b * s
    alpha = (2 * depth) ** 0.25
    row = lambda v: v.reshape(1, -1)

    proj = _matmul(x.reshape(t, d), _regroup_w_in(w_in), 1024, 1280).reshape(b, s, PROJ_WIDTH)
    memkv = _matmul(mem.reshape(-1, d), w_mem_kv, 1024, 512).reshape(b, mem.shape[1], 2 * MEM_WIDTH)
    cmp_kv = _compress(proj, jnp.stack([cmp_pos_k, cmp_pos_v]), jnp.stack([cmp_w1_k, cmp_w1_v]),
                       jnp.stack([cmp_w2_k, cmp_w2_v]))
    onsa = _nsa(proj, cmp_kv)
    h, idx_t, _, grow = _mix(alpha, onsa, proj, memkv, pool_w, row(pool_scale), row(gn_nsa), row(gn_pool),
                             row(gn_mem), w_out.astype(BF16), x, row(ln1_g), row(ln1_b),
                             w_router.T, b_router.reshape(-1, 1))

    dest, cnt = _route(idx_t)
    counts = cnt[:, 0].astype(I32)
    n_pad = t * TOP_K + N_EXPERTS * MOE_SUB
    n_pad = (n_pad + MOE_TM - 1) // MOE_TM * MOE_TM
    xs = _dispatch(dest, h, counts, n_pad)
    items = _expert_items(counts, n_pad // MOE_TM)
    ys = _experts(items, xs, _degate(w_gu), _degate_bias(b_gu), w_dn.astype(BF16), b_dn[:, None, :])
    out = _combine(alpha, dest, ys, h, grow, row(ln2_g), row(ln2_b))
    return out.reshape(b, s, d)


def kernel(x, mem, w_in, cmp_pos_k, cmp_w1_k, cmp_w2_k, cmp_pos_v, cmp_w1_v, cmp_w2_v, pool_w, pool_scale,
           w_mem_kv, gn_nsa, gn_pool, gn_mem, w_out, ln1_g, ln1_b, w_router, b_router, w_gu, b_gu, w_dn, b_dn,
           ln2_g, ln2_b):
    depth = w_in.shape[0]
    for l in range(depth):
        x = _layer(depth, x, mem, w_in[l], cmp_pos_k[l], cmp_w1_k[l], cmp_w2_k[l], cmp_pos_v[l], cmp_w1_v[l],
                   cmp_w2_v[l], pool_w[l], pool_scale[l], w_mem_kv[l], gn_nsa[l], gn_pool[l], gn_mem[l],
                   w_out[l], ln1_g[l], ln1_b[l], w_router[l], b_router[l], w_gu[l], b_gu[l], w_dn[l], b_dn[l],
                   ln2_g[l], ln2_b[l])
    return x
```
